```python
import jax, jax.numpy as jnp
from jax import lax
import numpy as np

D_MODEL = 1024
BATCH = 2
SEQ = 16384
DEPTH = 2

GRID_W = 64
CTX_LEN = 256
MLA_HEADS = 8
Q_LORA = 256
KV_LORA = 128
NOPE_DIM = 64
ROPE_DIM = 32
V_DIM = 64
ROPE_THETA = 10000.0
Q_BLOCK = 128
SSD_HEADS = 8
SSD_HEAD_DIM = 64
SSD_INNER = SSD_HEADS * SSD_HEAD_DIM
SSD_GROUPS = 2
SSD_STATE = 64
CONV_W = 5
CHUNK = 128
MLA_COLS = Q_LORA + KV_LORA + ROPE_DIM
XBC_DIM = SSD_INNER + 2 * SSD_GROUPS * SSD_STATE
SSD_COLS = SSD_INNER + XBC_DIM + 2 * SSD_HEADS
IN_COLS = MLA_COLS + SSD_COLS
MIX_WIDTH = MLA_HEADS * V_DIM + SSD_INNER
N_GROUPS = 4
EXPERTS_PER_GROUP = 8
N_EXPERTS = N_GROUPS * EXPERTS_PER_GROUP
TOP_K_INNER = 2
EXPERT_FF = 256
EPS = 1e-6

kernel_name = "hybrid_mla_ssd_hmoe_prefix_dit"


def rmsnorm(x, g):
    xf = x.astype(jnp.float32)
    y = xf * lax.rsqrt(jnp.mean(xf * xf, axis=-1, keepdims=True) + EPS) * g.astype(jnp.float32)
    return y.astype(x.dtype)


def modnorm(x, g, shift, scale):
    return rmsnorm(x, g) * (1 + scale) + shift


def rope2d(x, cos, sin):
    x1, x2 = jnp.split(x, 2, axis=-1)
    return jnp.concatenate([x1 * cos - x2 * sin, x2 * cos + x1 * sin], axis=-1).astype(x.dtype)


def mla_heads(p, q_norm_g, w_qb, kv_norm_g, w_kvb, cos, sin, want_q):
    b, L = p.shape[:2]
    qa, kva, kr = jnp.split(p, [Q_LORA, Q_LORA + KV_LORA], axis=-1)
    kv = (rmsnorm(kva, kv_norm_g) @ w_kvb).reshape(b, L, MLA_HEADS, NOPE_DIM + V_DIM)
    k_nope, v = jnp.split(kv, [NOPE_DIM], axis=-1)
    k_rope = kr if cos is None else rope2d(kr, cos, sin)
    if not want_q:
        return None, None, k_nope, k_rope, v
    q = (rmsnorm(qa, q_norm_g) @ w_qb).reshape(b, L, MLA_HEADS, NOPE_DIM + ROPE_DIM)
    q_nope, q_rope = jnp.split(q, [NOPE_DIM], axis=-1)
    if cos is not None:
        q_rope = rope2d(q_rope, cos[:, None, :], sin[:, None, :])
    return q_nope, q_rope, k_nope, k_rope, v


def block_attention(q_nope, q_rope, k_nope, k_rope, v):
    b, Lq, H, _ = q_nope.shape
    nb = Lq // Q_BLOCK
    scale = (NOPE_DIM + ROPE_DIM) ** -0.5
    qn = jnp.moveaxis(q_nope.reshape(b, nb, Q_BLOCK, H, NOPE_DIM), 1, 0)
    qr = jnp.moveaxis(q_rope.reshape(b, nb, Q_BLOCK, H, ROPE_DIM), 1, 0)

    def one_block(args):
        qn_b, qr_b = args
        s = jnp.einsum('bqhd,bkhd->bhqk', qn_b, k_nope) + jnp.einsum('bqhr,bkr->bhqk', qr_b, k_rope)
        p = jax.nn.softmax(s.astype(jnp.float32) * scale, axis=-1).astype(v.dtype)
        return jnp.einsum('bhqk,bkhd->bqhd', p, v)

    out = lax.map(one_block, (qn, qr))
    return jnp.moveaxis(out, 0, 1).reshape(b, Lq, H * V_DIM)


def dwconv(x, w, bias):
    y = lax.conv_general_dilated(x, w[:, None, :], window_strides=(1,),
                                 padding=[(CONV_W // 2, CONV_W // 2)],
                                 dimension_numbers=('NWC', 'WIO', 'NWC'),
                                 feature_group_count=x.shape[-1])
    return y + bias


def ssd_scan(x, dt, A, Bm, Cm, h0, want_y):
    b, L, H, P = x.shape
    G, N = Bm.shape[2], Bm.shape[3]
    nc = L // CHUNK
    Bh = jnp.repeat(Bm, H // G, axis=2).reshape(b, nc, CHUNK, H, N)
    Ch = jnp.repeat(Cm, H // G, axis=2).reshape(b, nc, CHUNK, H, N)
    dt = dt.reshape(b, nc, CHUNK, H)
    xdt = x.reshape(b, nc, CHUNK, H, P) * dt[..., None]
    acs = jnp.cumsum(dt * A, axis=2)
    a_tot = acs[:, :, -1]
    states = jnp.einsum('bcqhn,bcqh,bcqhp->bchpn', Bh, jnp.exp(a_tot[:, :, None] - acs), xdt)

    def step(h, inp):
        st, at = inp
        return jnp.exp(at)[..., None, None] * h + st, h

    h_final, h_prev = lax.scan(step, h0, (jnp.moveaxis(states, 1, 0), jnp.moveaxis(a_tot, 1, 0)))
    if not want_y:
        return None, h_final
    h_prev = jnp.moveaxis(h_prev, 0, 1)
    seg = acs[:, :, :, None, :] - acs[:, :, None, :, :]
    lower = jnp.tril(jnp.ones((CHUNK, CHUNK), bool))[None, None, :, :, None]
    decay = jnp.exp(jnp.where(lower, seg, -jnp.inf))
    scores = jnp.einsum('bcihn,bcjhn->bcijh', Ch, Bh) * decay
    y = (jnp.einsum('bcijh,bcjhp->bcihp', scores, xdt)
         + jnp.einsum('bcihn,bchpn->bcihp', Ch * jnp.exp(acs)[..., None], h_prev))
    return y.reshape(b, L, H, P), h_final


def ssd_mixer(p_lat, p_ctx, conv_w, conv_b, a_log, dt_bias, d_skip, norm_g, ctx_out):
    def prep(p):
        b, L = p.shape[:2]
        z, xbc, dtr = jnp.split(p, [SSD_INNER, SSD_INNER + XBC_DIM], axis=-1)
        xbc = jax.nn.silu(dwconv(xbc, conv_w, conv_b))
        xs, Bm, Cm = jnp.split(xbc, [SSD_INNER, SSD_INNER + SSD_GROUPS * SSD_STATE], axis=-1)
        xs = xs.reshape(b, L, SSD_HEADS, SSD_HEAD_DIM)
        Bm = Bm.reshape(b, L, SSD_GROUPS, SSD_STATE)
        Cm = Cm.reshape(b, L, SSD_GROUPS, SSD_STATE)
        dt = jax.nn.softplus(dtr.reshape(b, L, 2, SSD_HEADS).astype(jnp.float32) + dt_bias.astype(jnp.float32))
        return z, xs, Bm, Cm, dt

    A = -jnp.exp(a_log.astype(jnp.float32))
    zl, xl, Bl, Cl, dtl = prep(p_lat)
    zc, xc, Bc, Cc, dtc = prep(p_ctx)
    b = p_lat.shape[0]
    h0 = jnp.zeros((b, SSD_HEADS, SSD_HEAD_DIM, SSD_STATE), jnp.float32)
    y_lat = d_skip[:, None] * xl
    y_ctx = d_skip[:, None] * xc if ctx_out else None
    for d in range(2):
        fl = (lambda t: jnp.flip(t, axis=1)) if d == 1 else (lambda t: t)
        yc, hc = ssd_scan(fl(xc), fl(dtc[:, :, d]), A[d], fl(Bc), fl(Cc), h0, ctx_out)
        yl, _ = ssd_scan(fl(xl), fl(dtl[:, :, d]), A[d], fl(Bl), fl(Cl), hc, True)
        y_lat = y_lat + fl(yl)
        if ctx_out:
            y_ctx = y_ctx + fl(yc)

    def gated_norm(y, z):
        y = y.reshape(z.shape).astype(z.dtype) * jax.nn.silu(z)
        return rmsnorm(y, norm_g)

    return gated_norm(y_lat, zl), (gated_norm(y_ctx, zc) if ctx_out else None)


def mixer(h_lat, h_ctx, w_in, q_norm_g, w_qb, kv_norm_g, w_kvb, conv_w, conv_b, a_log, dt_bias,
          d_skip, ssd_norm_g, w_out, cos, sin, ctx_out):
    p_lat = h_lat @ w_in
    p_ctx = h_ctx @ w_in
    m_lat, s_lat = jnp.split(p_lat, [MLA_COLS], axis=-1)
    m_ctx, s_ctx = jnp.split(p_ctx, [MLA_COLS], axis=-1)
    qn_c, qr_c, kn_c, kr_c, v_c = mla_heads(m_ctx, q_norm_g, w_qb, kv_norm_g, w_kvb, None, None, ctx_out)
    qn_l, qr_l, kn_l, kr_l, v_l = mla_heads(m_lat, q_norm_g, w_qb, kv_norm_g, w_kvb, cos, sin, True)
    att_lat = block_attention(qn_l, qr_l,
                              jnp.concatenate([kn_c, kn_l], axis=1),
                              jnp.concatenate([kr_c, kr_l], axis=1),
                              jnp.concatenate([v_c, v_l], axis=1))
    ssd_lat, ssd_ctx = ssd_mixer(s_lat, s_ctx, conv_w, conv_b, a_log, dt_bias, d_skip, ssd_norm_g, ctx_out)
    o_lat = jnp.concatenate([att_lat, ssd_lat.astype(att_lat.dtype)], axis=-1) @ w_out
    if not ctx_out:
        return o_lat, None
    att_ctx = block_attention(qn_c, qr_c, kn_c, kr_c, v_c)
    o_ctx = jnp.concatenate([att_ctx, ssd_ctx.astype(att_ctx.dtype)], axis=-1) @ w_out
    return o_lat, o_ctx


def hier_moe(h, rw1, rb1, rw2, rb2, w_gate, w_up, w_down):
    n = h.shape[0]
    p1 = jax.nn.softmax((h @ rw1 + rb1).astype(jnp.float32), axis=-1)
    p_top, g_idx = lax.top_k(p1, 1)
    logit2 = (h @ rw2 + rb2).astype(jnp.float32).reshape(n, N_GROUPS, EXPERTS_PER_GROUP)
    l_sel = jnp.take_along_axis(logit2, g_idx[:, :, None], axis=1)[:, 0]
    v2, i2 = lax.top_k(jax.nn.softmax(l_sel, axis=-1), TOP_K_INNER)
    v2 = v2 / jnp.sum(v2, axis=-1, keepdims=True)
    within = jnp.sum(jax.nn.one_hot(i2, EXPERTS_PER_GROUP, dtype=jnp.float32) * v2[..., None], axis=1)
    gate = (jax.nn.one_hot(g_idx[:, 0], N_GROUPS, dtype=jnp.float32) * p_top)[:, :, None] * within[:, None, :]
    gate = gate.reshape(n, N_EXPERTS).astype(h.dtype)
    y = jnp.zeros_like(h)
    for e in range(N_EXPERTS):
        y = y + gate[:, e:e + 1] * ((jax.nn.silu(h @ w_gate[e]) * (h @ w_up[e])) @ w_down[e])
    return y


def setup_inputs(seed: int = 0) -> dict:
    key = jax.random.key(seed)
    ks = jax.random.split(key, 32)

    def nrm(k, shape, scale):
        return scale * jax.random.normal(k, shape, jnp.float32)

    dt0 = jnp.exp(jax.random.uniform(ks[17], (DEPTH, 2, SSD_HEADS), jnp.float32, np.log(1e-3), np.log(1e-1)))
    return {
        "x": nrm(ks[0], (BATCH, SEQ, D_MODEL), 1.0),
        "c": nrm(ks[1], (BATCH, D_MODEL), 1.0),
        "ctx": nrm(ks[2], (BATCH, CTX_LEN, D_MODEL), 1.0),
        "c_ctx": nrm(ks[3], (D_MODEL,), 1.0),
        "w_mod": nrm(ks[4], (DEPTH, D_MODEL, 6 * D_MODEL), 0.5 * D_MODEL ** -0.5),
        "b_mod": nrm(ks[5], (DEPTH, 6 * D_MODEL), 0.02),
        "norm1_g": 1.0 + nrm(ks[6], (DEPTH, D_MODEL), 0.02),
        "norm2_g": 1.0 + nrm(ks[7], (DEPTH, D_MODEL), 0.02),
        "w_in": nrm(ks[8], (DEPTH, D_MODEL, IN_COLS), D_MODEL ** -0.5),
        "q_norm_g": 1.0 + nrm(ks[9], (DEPTH, Q_LORA), 0.02),
        "w_qb": nrm(ks[10], (DEPTH, Q_LORA, MLA_HEADS * (NOPE_DIM + ROPE_DIM)), Q_LORA ** -0.5),
        "kv_norm_g": 1.0 + nrm(ks[11], (DEPTH, KV_LORA), 0.02),
        "w_kvb": nrm(ks[12], (DEPTH, KV_LORA, MLA_HEADS * (NOPE_DIM + V_DIM)), KV_LORA ** -0.5),
        "conv_w": nrm(ks[13], (DEPTH, CONV_W, XBC_DIM), CONV_W ** -0.5),
        "conv_b": nrm(ks[14], (DEPTH, XBC_DIM), 0.02),
        "a_log": jnp.log(jax.random.uniform(ks[15], (DEPTH, 2, SSD_HEADS), jnp.float32, 1.0, 16.0)),
        "dt_bias": dt0 + jnp.log(-jnp.expm1(-dt0)),
        "d_skip": 1.0 + nrm(ks[16], (DEPTH, SSD_HEADS), 0.1),
        "ssd_norm_g": 1.0 + nrm(ks[18], (DEPTH, SSD_INNER), 0.02),
        "w_out": nrm(ks[19], (DEPTH, MIX_WIDTH, D_MODEL), MIX_WIDTH ** -0.5),
        "router_w1": nrm(ks[20], (DEPTH, D_MODEL, N_GROUPS), D_MODEL ** -0.5),
        "router_b1": nrm(ks[21], (DEPTH, N_GROUPS), 0.01),
        "router_w2": nrm(ks[22], (DEPTH, D_MODEL, N_EXPERTS), D_MODEL ** -0.5),
        "router_b2": nrm(ks[23], (DEPTH, N_EXPERTS), 0.01),
        "w_gate": nrm(ks[24], (DEPTH, N_EXPERTS, D_MODEL, EXPERT_FF), D_MODEL ** -0.5),
        "w_up": nrm(ks[25], (DEPTH, N_EXPERTS, D_MODEL, EXPERT_FF), D_MODEL ** -0.5),
        "w_down": nrm(ks[26], (DEPTH, N_EXPERTS, EXPERT_FF, D_MODEL), EXPERT_FF ** -0.5),
        "final_g": 1.0 + nrm(ks[27], (D_MODEL,), 0.02),
    }


def reference(x, c, ctx, c_ctx, w_mod, b_mod, norm1_g, norm2_g, w_in, q_norm_g, w_qb, kv_norm_g, w_kvb,
              conv_w, conv_b, a_log, dt_bias, d_skip, ssd_norm_g, w_out, router_w1, router_b1,
              router_w2, router_b2, w_gate, w_up, w_down, final_g):
    b, L, D = x.shape
    rows = L // GRID_W
    row = jnp.repeat(jnp.arange(rows), GRID_W)
    col = jnp.tile(jnp.arange(GRID_W), rows)
    inv_freq = ROPE_THETA ** (-jnp.arange(ROPE_DIM // 4, dtype=jnp.float32) / (ROPE_DIM // 4))
    ang = jnp.concatenate([row[:, None] * inv_freq, col[:, None] * inv_freq], axis=-1)
    cos, sin = jnp.cos(ang), jnp.sin(ang)

    xl, xc = x, ctx
    n_ctx_tok = ctx.shape[0] * ctx.shape[1]
    for l in range(DEPTH):
        ctx_out = l < DEPTH - 1
        mod = jax.nn.silu(c) @ w_mod[l] + b_mod[l]
        sh1, sc1, g1, sh2, sc2, g2 = [m[:, None, :] for m in jnp.split(mod, 6, axis=-1)]
        modc = jax.nn.silu(c_ctx) @ w_mod[l] + b_mod[l]
        sh1c, sc1c, g1c, sh2c, sc2c, g2c = jnp.split(modc, 6)
        h_l = modnorm(xl, norm1_g[l], sh1, sc1)
        h_c = modnorm(xc, norm1_g[l], sh1c, sc1c)
        o_l, o_c = mixer(h_l, h_c, w_in[l], q_norm_g[l], w_qb[l], kv_norm_g[l], w_kvb[l], conv_w[l],
                         conv_b[l], a_log[l], dt_bias[l], d_skip[l], ssd_norm_g[l], w_out[l], cos, sin, ctx_out)
        xl = xl + g1 * o_l
        h2 = modnorm(xl, norm2_g[l], sh2, sc2)
        moe_args = (router_w1[l], router_b1[l], router_w2[l], router_b2[l], w_gate[l], w_up[l], w_down[l])
        if ctx_out:
            xc = xc + g1c * o_c
            h2c = modnorm(xc, norm2_g[l], sh2c, sc2c)
            y = hier_moe(jnp.concatenate([h2c.reshape(-1, D), h2.reshape(-1, D)], axis=0), *moe_args)
            xc = xc + g2c * y[:n_ctx_tok].reshape(xc.shape)
            xl = xl + g2 * y[n_ctx_tok:].reshape(xl.shape)
        else:
            xl = xl + g2 * hier_moe(h2.reshape(-1, D), *moe_args).reshape(xl.shape)
    return rmsnorm(xl, final_g)
```

```python
import functools

import jax
import jax.numpy as jnp
from jax import lax
from jax.experimental import pallas as pl
from jax.experimental.pallas import tpu as pltpu

F32 = jnp.float32
BF16 = jnp.bfloat16
HIGHEST = lax.Precision.HIGHEST

GRID_W = 64
MLA_HEADS = 8
Q_LORA = 256
KV_LORA = 128
NOPE_DIM = 64
ROPE_DIM = 32
V_DIM = 64
ROPE_THETA = 10000.0
SSD_HEADS = 8
SSD_HEAD_DIM = 64
SSD_INNER = SSD_HEADS * SSD_HEAD_DIM
SSD_GROUPS = 2
SSD_STATE = 64
CONV_W = 5
CHUNK = 128
XBC_DIM = SSD_INNER + 2 * SSD_GROUPS * SSD_STATE
N_GROUPS = 4
EXPERTS_PER_GROUP = 8
N_EXPERTS = N_GROUPS * EXPERTS_PER_GROUP
EXPERT_FF = 256
EPS = 1e-6

LANES = 128
SUBLANES = 8
TOKEN_TILE = 256
KQ_DIM = KV_LORA + ROPE_DIM
ATT_SCALE = (NOPE_DIM + ROPE_DIM) ** -0.5
NEG_BIG = -1e30

COL_QA = 0
COL_KVA = COL_QA + Q_LORA
COL_Z = COL_KVA + KV_LORA
COL_XBC = COL_Z + SSD_INNER
COL_MISC = COL_XBC + XBC_DIM
IN_COLS_PAD = COL_MISC + LANES
MISC_KR = 2 * SSD_HEADS
MISC_KRP = MISC_KR + ROPE_DIM


def _params(sem, vmem_mb=48):
    return pltpu.CompilerParams(dimension_semantics=sem, vmem_limit_bytes=vmem_mb * 1024 * 1024)


def _pick(n, cands):
    for c in cands:
        if n % c == 0:
            return c
    raise ValueError(f"no tile for {n} in {cands}")


def _rms(x, g):
    return x * lax.rsqrt(jnp.mean(x * x, axis=-1, keepdims=True) + EPS) * g


def _silu(x):
    return x * (1.0 / (1.0 + jnp.exp(-x)))


def _bdot(a, b):
    return jnp.dot(a.astype(BF16), b.astype(BF16), preferred_element_type=F32)


def _mod_kernel(c_ref, w_ref, b_ref, o_ref):
    s = _silu(c_ref[...])
    o_ref[0] = jnp.dot(s, w_ref[0], precision=HIGHEST, preferred_element_type=F32) + b_ref[0]


def _modulation(cvec, w_mod, b_mod):
    depth, d, n = w_mod.shape
    tn = _pick(n, (768, 512, 256, 128))
    return pl.pallas_call(
        _mod_kernel,
        grid=(depth, n // tn),
        in_specs=[pl.BlockSpec((SUBLANES, d), lambda l, j: (0, 0)),
                  pl.BlockSpec((1, d, tn), lambda l, j: (l, 0, j)),
                  pl.BlockSpec((1, 1, tn), lambda l, j: (l, 0, j))],
        out_specs=pl.BlockSpec((1, SUBLANES, tn), lambda l, j: (l, 0, j)),
        out_shape=jax.ShapeDtypeStruct((depth, SUBLANES, n), F32),
        compiler_params=_params(("parallel", "parallel")),
        name="modulation",
    )(cvec, w_mod, b_mod.reshape(depth, 1, n))


def _inproj_kernel(x_ref, mod_ref, g_ref, win_ref, qg_ref, kvg_ref, wq_ref, wkbt_ref, tab_ref,
                   q_ref, kv_ref, z_ref, xbc_ref, misc_ref):
    x = x_ref[0]
    shift = mod_ref[0, 0, 0:1, :]
    scale = mod_ref[0, 0, 1:2, :]
    h = _rms(x, g_ref[...]) * (1.0 + scale) + shift
    p = _bdot(h, win_ref[...])
    z_ref[0] = p[:, COL_Z:COL_XBC]
    xbc_ref[0] = p[:, COL_XBC:COL_MISC]
    misc = p[:, COL_MISC:IN_COLS_PAD]
    misc_ref[0] = misc

    tab = tab_ref[...]
    rope_c = tab[:, :MLA_HEADS * ROPE_DIM]
    rope_s = tab[:, MLA_HEADS * ROPE_DIM:]

    ckv = _rms(p[:, COL_KVA:COL_Z], kvg_ref[...])
    k_rope = (misc[:, MISC_KR:MISC_KR + ROPE_DIM] * rope_c[:, :ROPE_DIM]
              + misc[:, MISC_KRP:MISC_KRP + ROPE_DIM] * rope_s[:, :ROPE_DIM])
    kv_ref[0, :, 0:KV_LORA] = ckv.astype(BF16)
    kv_ref[0, :, KV_LORA:KQ_DIM] = k_rope.astype(BF16)

    qn = _rms(p[:, COL_QA:COL_KVA], qg_ref[...])
    q3 = _bdot(qn, wq_ref[...])
    n_nope = MLA_HEADS * NOPE_DIM
    n_rope = MLA_HEADS * ROPE_DIM
    q_nope = (q3[:, :n_nope] * ATT_SCALE).astype(BF16)
    q_rope = ((q3[:, n_nope:n_nope + n_rope] * rope_c
               + q3[:, n_nope + n_rope:] * rope_s) * ATT_SCALE).astype(BF16)
    for hd in range(MLA_HEADS):
        q_abs = jnp.dot(q_nope[:, hd * NOPE_DIM:(hd + 1) * NOPE_DIM], wkbt_ref[hd],
                        preferred_element_type=F32)
        q_ref[0, hd, :, 0:KV_LORA] = q_abs.astype(BF16)
        q_ref[0, hd, :, KV_LORA:KQ_DIM] = q_rope[:, hd * ROPE_DIM:(hd + 1) * ROPE_DIM]


def _inproj(xc, modsel, g1, win, qg, kvg, wq, wkbt, tab, n_lat_tiles):
    b, lc, d = xc.shape
    tm = TOKEN_TILE
    nt = lc // tm
    const2 = lambda bi, i: (0, 0)
    return pl.pallas_call(
        _inproj_kernel,
        grid=(b, nt),
        in_specs=[pl.BlockSpec((1, tm, d), lambda bi, i: (bi, i, 0)),
                  pl.BlockSpec((1, 1, SUBLANES, d), lambda bi, i: (bi, i // n_lat_tiles, 0, 0)),
                  pl.BlockSpec((1, d), const2),
                  pl.BlockSpec((d, IN_COLS_PAD), const2),
                  pl.BlockSpec((1, Q_LORA), const2),
                  pl.BlockSpec((1, KV_LORA), const2),
                  pl.BlockSpec(wq.shape, const2),
                  pl.BlockSpec(wkbt.shape, lambda bi, i: (0, 0, 0)),
                  pl.BlockSpec((tm, tab.shape[1]), lambda bi, i: (i, 0))],
        out_specs=[pl.BlockSpec((1, MLA_HEADS, tm, KQ_DIM), lambda bi, i: (bi, 0, i, 0)),
                   pl.BlockSpec((1, tm, KQ_DIM), lambda bi, i: (bi, i, 0)),
                   pl.BlockSpec((1, tm, SSD_INNER), lambda bi, i: (bi, i, 0)),
                   pl.BlockSpec((1, tm, XBC_DIM), lambda bi, i: (bi, i, 0)),
                   pl.BlockSpec((1, tm, LANES), lambda bi, i: (bi, i, 0))],
        out_shape=[jax.ShapeDtypeStruct((b, MLA_HEADS, lc, KQ_DIM), BF16),
                   jax.ShapeDtypeStruct((b, lc, KQ_DIM), BF16),
                   jax.ShapeDtypeStruct((b, lc, SSD_INNER), F32),
                   jax.ShapeDtypeStruct((b, lc, XBC_DIM), F32),
                   jax.ShapeDtypeStruct((b, lc, LANES), F32)],
        compiler_params=_params(("parallel", "parallel")),
        name="inproj",
    )(xc, modsel, g1, win, qg, kvg, wq, wkbt, tab)


def _attn_kernel(q_ref, kv_ref, wvb_ref, o_ref, m_sc, l_sc, acc_sc, *, tq, tk, nk, n_lat_q, ctx_len):
    qi = pl.program_id(1)
    kj = pl.program_id(2)
    rows = MLA_HEADS * tq

    @pl.when(kj == 0)
    def _():
        m_sc[...] = jnp.full(m_sc.shape, NEG_BIG, F32)
        l_sc[...] = jnp.zeros(l_sc.shape, F32)
        acc_sc[...] = jnp.zeros(acc_sc.shape, F32)

    def step(ctx_only):
        q = q_ref[0].reshape(rows, KQ_DIM)
        kv = kv_ref[0]
        s = lax.dot_general(q, kv, (((1,), (1,)), ((), ())), preferred_element_type=F32)
        if ctx_only:
            col = lax.broadcasted_iota(jnp.int32, s.shape, 1)
            s = jnp.where(col >= tk - ctx_len, s, NEG_BIG)
        m_prev = m_sc[...]
        m_new = jnp.maximum(m_prev, jnp.max(s, axis=-1, keepdims=True))
        alpha = jnp.exp(m_prev - m_new)
        p = jnp.exp(s - m_new)
        l_sc[...] = alpha * l_sc[...] + jnp.sum(p, axis=-1, keepdims=True)
        acc_sc[...] = alpha * acc_sc[...] + jnp.dot(p.astype(BF16), kv[:, :KV_LORA],
                                                    preferred_element_type=F32)
        m_sc[...] = m_new

    is_ctx_q = qi >= n_lat_q
    pl.when(jnp.logical_not(is_ctx_q))(functools.partial(step, False))
    pl.when(jnp.logical_and(is_ctx_q, kj == nk - 1))(functools.partial(step, True))

    @pl.when(kj == nk - 1)
    def _():
        o = (acc_sc[...] / l_sc[...]).astype(BF16)
        outs = [jnp.dot(o[hd * tq:(hd + 1) * tq], wvb_ref[hd], preferred_element_type=F32)
                for hd in range(MLA_HEADS)]
        o_ref[0] = jnp.concatenate(outs, axis=-1).astype(BF16)


def _attention(q, kv, wvb, n_lat_q, ctx_len):
    b, _, lc, _ = q.shape
    tq = TOKEN_TILE
    tk = _pick(lc, (1280, 768, 512, 256))
    assert ctx_len <= tk and ctx_len == TOKEN_TILE
    nq, nk = lc // tq, lc // tk
    rows = MLA_HEADS * tq
    kern = functools.partial(_attn_kernel, tq=tq, tk=tk, nk=nk, n_lat_q=n_lat_q, ctx_len=ctx_len)
    return pl.pallas_call(
        kern,
        grid=(b, nq, nk),
        in_specs=[pl.BlockSpec((1, MLA_HEADS, tq, KQ_DIM), lambda bi, i, j: (bi, 0, i, 0)),
                  pl.BlockSpec((1, tk, KQ_DIM), lambda bi, i, j: (bi, j, 0)),
                  pl.BlockSpec(wvb.shape, lambda bi, i, j: (0, 0, 0))],
        out_specs=pl.BlockSpec((1, tq, MLA_HEADS * V_DIM), lambda bi, i, j: (bi, i, 0)),
        out_shape=jax.ShapeDtypeStruct((b, lc, MLA_HEADS * V_DIM), BF16),
        scratch_shapes=[pltpu.VMEM((rows, 1), F32), pltpu.VMEM((rows, 1), F32),
                        pltpu.VMEM((rows, KV_LORA), F32)],
        compiler_params=_params(("parallel", "parallel", "arbitrary")),
        name="attention",
    )(q, kv, wvb)


def _conv_kernel(x_ref, prev_ref, next_ref, w_ref, b_ref, o_ref, ext_sc, *, tm, n_lat_tiles):
    i = pl.program_id(1)
    halo = SUBLANES
    prev_ok = jnp.logical_and(i >= 1, i < n_lat_tiles)
    next_ok = i < n_lat_tiles - 1
    ext_sc[0:halo, :] = jnp.where(prev_ok, prev_ref[0], 0.0)
    ext_sc[halo:halo + tm, :] = x_ref[0]
    ext_sc[halo + tm:2 * halo + tm, :] = jnp.where(next_ok, next_ref[0], 0.0)
    acc = jnp.broadcast_to(b_ref[...], (tm, XBC_DIM))
    for k in range(CONV_W):
        off = halo - CONV_W // 2 + k
        acc = acc + w_ref[k:k + 1, :] * ext_sc[off:off + tm, :]
    o_ref[0] = _silu(acc)


def _conv(xbc, conv_w, conv_b, n_lat_tiles):
    b, lc, c = xbc.shape
    tm = TOKEN_TILE
    nt = lc // tm
    hb = tm // SUBLANES
    last = lc // SUBLANES - 1
    kern = functools.partial(_conv_kernel, tm=tm, n_lat_tiles=n_lat_tiles)
    return pl.pallas_call(
        kern,
        grid=(b, nt),
        in_specs=[pl.BlockSpec((1, tm, c), lambda bi, i: (bi, i, 0)),
                  pl.BlockSpec((1, SUBLANES, c), lambda bi, i: (bi, jnp.maximum(i * hb - 1, 0), 0)),
                  pl.BlockSpec((1, SUBLANES, c), lambda bi, i: (bi, jnp.minimum((i + 1) * hb, last), 0)),
                  pl.BlockSpec((SUBLANES, c), lambda bi, i: (0, 0)),
                  pl.BlockSpec((1, c), lambda bi, i: (0, 0))],
        out_specs=pl.BlockSpec((1, tm, c), lambda bi, i: (bi, i, 0)),
        out_shape=jax.ShapeDtypeStruct((b, lc, c), F32),
        scratch_shapes=[pltpu.VMEM((tm + 2 * SUBLANES, c), F32)],
        compiler_params=_params(("parallel", "parallel")),
        name="conv",
    )(xbc, xbc, xbc, conv_w, conv_b)


def _ssd_kernel(xbc_ref, misc_ref, bias_ref, alog_ref, dskip_ref, y_ref, h_sc, *, direction):
    i = pl.program_id(1)
    q = CHUNK
    n = SSD_STATE
    pdim = SSD_HEAD_DIM

    @pl.when(i == 0)
    def _():
        h_sc[...] = jnp.zeros(h_sc.shape, F32)

    xbc = xbc_ref[0]
    xs = xbc[:, :SSD_INNER]
    bmat = xbc[:, SSD_INNER:SSD_INNER + SSD_GROUPS * n]
    cmat = xbc[:, SSD_INNER + SSD_GROUPS * n:]
    raw = misc_ref[0] + bias_ref[...]
    dt = jnp.maximum(raw, 0.0) + jnp.log(1.0 + jnp.exp(-jnp.abs(raw)))
    lane = lax.broadcasted_iota(jnp.int32, (1, LANES), 1)
    a = dt * jnp.where(lane < 2 * SSD_HEADS, -jnp.exp(alog_ref[...]), 0.0)

    r_i = lax.broadcasted_iota(jnp.int32, (q, q), 0)
    c_i = lax.broadcasted_iota(jnp.int32, (q, q), 1)
    valid = (c_i <= r_i) if direction == 0 else (c_i >= r_i)
    tri = jnp.where(valid, 1.0, 0.0).astype(F32)
    acs = jnp.dot(tri, a, precision=HIGHEST, preferred_element_type=F32)
    acs_t = acs.T
    bt = bmat.T
    last = q - 1 if direction == 0 else 0

    cb = [jnp.dot(cmat[:, g * n:(g + 1) * n].astype(BF16), bt[g * n:(g + 1) * n, :].astype(BF16),
                  preferred_element_type=F32) for g in range(SSD_GROUPS)]
    ys = []
    for hd in range(SSD_HEADS):
        j = direction * SSD_HEADS + hd
        g = hd // (SSD_HEADS // SSD_GROUPS)
        col = acs[:, j:j + 1]
        row = acs_t[j:j + 1, :]
        atot = acs_t[j:j + 1, last:last + 1]
        decay = jnp.exp(jnp.where(valid, col - row, NEG_BIG))
        xh = xs[:, hd * pdim:(hd + 1) * pdim]
        xdt = (xh * dt[:, j:j + 1]).astype(BF16)
        h_prev = h_sc[hd]
        c_g = cmat[:, g * n:(g + 1) * n]
        y = (jnp.dot((cb[g] * decay).astype(BF16), xdt, preferred_element_type=F32)
             + jnp.dot((c_g * jnp.exp(col)).astype(BF16), h_prev.astype(BF16),
                       preferred_element_type=F32))
        if direction == 0:
            y = y + dskip_ref[:, hd * pdim:(hd + 1) * pdim] * xh
        ys.append(y)
        bw = (bt[g * n:(g + 1) * n, :] * jnp.exp(atot - row)).astype(BF16)
        h_sc[hd] = jnp.exp(atot) * h_prev + jnp.dot(bw, xdt, preferred_element_type=F32)
    y_ref[0] = jnp.concatenate(ys, axis=-1)


def _ssd(xbc_act, misc, bias_pad, alog_pad, dskip_wide, direction, n_lat_chunks):
    b, lc, c = xbc_act.shape
    nc = lc // CHUNK
    if direction == 0:
        cmap = lambda bi, i: (bi, (i + n_lat_chunks) % nc, 0)
    else:
        cmap = lambda bi, i: (bi, nc - 1 - i, 0)
    const2 = lambda bi, i: (0, 0)
    kern = functools.partial(_ssd_kernel, direction=direction)
    return pl.pallas_call(
        kern,
        grid=(b, nc),
        in_specs=[pl.BlockSpec((1, CHUNK, c), cmap),
                  pl.BlockSpec((1, CHUNK, LANES), cmap),
                  pl.BlockSpec((1, LANES), const2),
                  pl.BlockSpec((1, LANES), const2),
                  pl.BlockSpec((1, SSD_INNER), const2)],
        out_specs=pl.BlockSpec((1, CHUNK, SSD_INNER), cmap),
        out_shape=jax.ShapeDtypeStruct((b, lc, SSD_INNER), F32),
        scratch_shapes=[pltpu.VMEM((SSD_HEADS, SSD_STATE, SSD_HEAD_DIM), F32)],
        compiler_params=_params(("parallel", "arbitrary")),
        name=f"ssd_dir{direction}",
    )(xbc_act, misc, bias_pad, alog_pad, dskip_wide)


def _mixout_kernel(att_ref, y0_ref, y1_ref, z_ref, x_ref, mod_ref, sg_ref, wout_ref, g2_ref,
                   rw_ref, rb_ref, xn_ref, h2_ref, gate_ref):
    y = (y0_ref[0] + y1_ref[0]) * _silu(z_ref[0])
    ssd = _rms(y, sg_ref[...])
    half = MLA_HEADS * V_DIM
    o = (jnp.dot(att_ref[0], wout_ref[0:half, :], preferred_element_type=F32)
         + jnp.dot(ssd.astype(BF16), wout_ref[half:, :], preferred_element_type=F32))
    xn = x_ref[0] + mod_ref[0, 0, 2:3, :] * o
    xn_ref[0] = xn
    h2 = _rms(xn, g2_ref[...]) * (1.0 + mod_ref[0, 0, 4:5, :]) + mod_ref[0, 0, 3:4, :]
    h2_ref[0] = h2.astype(BF16)

    logits = jnp.dot(h2, rw_ref[...], precision=HIGHEST, preferred_element_type=F32) + rb_ref[...]
    lane = lax.broadcasted_iota(jnp.int32, logits.shape, 1)
    is_grp = jnp.logical_and(lane >= N_EXPERTS, lane < N_EXPERTS + N_GROUPS)
    l1 = jnp.where(is_grp, logits, NEG_BIG)
    m1 = jnp.max(l1, axis=-1, keepdims=True)
    p_top = 1.0 / jnp.sum(jnp.exp(l1 - m1), axis=-1, keepdims=True)
    g_idx = jnp.min(jnp.where(jnp.logical_and(is_grp, l1 == m1), lane - N_EXPERTS, N_GROUPS),
                    axis=-1, keepdims=True)
    sel = jnp.logical_and(lane < N_EXPERTS, (lane // EXPERTS_PER_GROUP) == g_idx)
    l2 = jnp.where(sel, logits, NEG_BIG)
    m2 = jnp.max(l2, axis=-1, keepdims=True)
    i_first = jnp.min(jnp.where(jnp.logical_and(sel, l2 == m2), lane, LANES), axis=-1, keepdims=True)
    l3 = jnp.where(lane == i_first, NEG_BIG, l2)
    m3 = jnp.max(l3, axis=-1, keepdims=True)
    i_second = jnp.min(jnp.where(jnp.logical_and(sel, l3 == m3), lane, LANES), axis=-1, keepdims=True)
    e2 = jnp.exp(m3 - m2)
    inv = p_top / (1.0 + e2)
    gate_ref[0] = jnp.where(lane == i_first, inv, jnp.where(lane == i_second, e2 * inv, 0.0))


def _mixout(att, y0, y1, z, xc, modsel, sg, wout, g2, rw, rb, n_lat_tiles):
    b, lc, d = xc.shape
    tm = TOKEN_TILE
    nt = lc // tm
    tok = lambda w: pl.BlockSpec((1, tm, w), lambda bi, i: (bi, i, 0))
    const2 = lambda bi, i: (0, 0)
    return pl.pallas_call(
        _mixout_kernel,
        grid=(b, nt),
        in_specs=[tok(SSD_INNER), tok(SSD_INNER), tok(SSD_INNER), tok(SSD_INNER), tok(d),
                  pl.BlockSpec((1, 1, SUBLANES, d), lambda bi, i: (bi, i // n_lat_tiles, 0, 0)),
                  pl.BlockSpec((1, SSD_INNER), const2),
                  pl.BlockSpec(wout.shape, const2),
                  pl.BlockSpec((1, d), const2),
                  pl.BlockSpec((d, LANES), const2),
                  pl.BlockSpec((1, LANES), const2)],
        out_specs=[tok(d), tok(d), tok(LANES)],
        out_shape=[jax.ShapeDtypeStruct((b, lc, d), F32),
                   jax.ShapeDtypeStruct((b, lc, d), BF16),
                   jax.ShapeDtypeStruct((b, lc, LANES), F32)],
        compiler_params=_params(("parallel", "parallel")),
        name="mixout",
    )(att, y0, y1, z, xc, modsel, sg, wout, g2, rw, rb)


def _moe_kernel(h2_ref, gate_ref, xn_ref, mod_ref, wg_ref, wu_ref, wd_ref, fg_ref, o_ref, acc_sc,
                *, tmm, seq, final):
    t = pl.program_id(1)
    e = pl.program_id(2)

    @pl.when(e == 0)
    def _():
        acc_sc[...] = jnp.zeros(acc_sc.shape, F32)

    h = h2_ref[0]
    gate = gate_ref[0]
    lane = lax.broadcasted_iota(jnp.int32, gate.shape, 1)
    gcol = jnp.sum(jnp.where(lane == e, gate, 0.0), axis=-1, keepdims=True)
    hg = jnp.dot(h, wg_ref[0], preferred_element_type=F32)
    hu = jnp.dot(h, wu_ref[0], preferred_element_type=F32)
    act = (_silu(hg) * hu * gcol).astype(BF16)
    acc_sc[...] += jnp.dot(act, wd_ref[0], preferred_element_type=F32)

    @pl.when(e == N_EXPERTS - 1)
    def _():
        row = t * tmm + lax.broadcasted_iota(jnp.int32, (tmm, 1), 0)
        g2 = jnp.where(row >= seq, mod_ref[0, 1, 5:6, :], mod_ref[0, 0, 5:6, :])
        out = xn_ref[0] + g2 * acc_sc[...]
        if final:
            out = _rms(out, fg_ref[...])
        o_ref[0] = out


def _moe(h2, gate, xn, modsel, wg, wu, wd, fg, seq, final):
    b, lc, d = xn.shape
    if final:
        rows_out = seq
        tmm = _pick(seq, (1024, 512, 256))
    else:
        rows_out = lc
        tmm = _pick(lc, (1280, 768, 512, 256))
    nt = rows_out // tmm
    kern = functools.partial(_moe_kernel, tmm=tmm, seq=seq, final=final)
    tok = lambda w: pl.BlockSpec((1, tmm, w), lambda bi, t, e: (bi, t, 0))
    return pl.pallas_call(
        kern,
        grid=(b, nt, N_EXPERTS),
        in_specs=[tok(d), tok(LANES), tok(d),
                  pl.BlockSpec((1, 2, SUBLANES, d), lambda bi, t, e: (bi, 0, 0, 0)),
                  pl.BlockSpec((1, d, EXPERT_FF), lambda bi, t, e: (e, 0, 0)),
                  pl.BlockSpec((1, d, EXPERT_FF), lambda bi, t, e: (e, 0, 0)),
                  pl.BlockSpec((1, EXPERT_FF, d), lambda bi, t, e: (e, 0, 0)),
                  pl.BlockSpec((1, d), lambda bi, t, e: (0, 0))],
        out_specs=tok(d),
        out_shape=jax.ShapeDtypeStruct((b, rows_out, d), F32),
        scratch_shapes=[pltpu.VMEM((tmm, d), F32)],
        compiler_params=_params(("parallel", "parallel", "arbitrary"), vmem_mb=56),
        name="moe_final" if final else "moe",
    )(h2, gate, xn, modsel, wg, wu, wd, fg)


def _rope_tables(seq, ctx_len):
    rows = seq // GRID_W
    row = jnp.repeat(jnp.arange(rows), GRID_W)
    col = jnp.tile(jnp.arange(GRID_W), rows)
    inv_freq = ROPE_THETA ** (-jnp.arange(ROPE_DIM // 4, dtype=F32) / (ROPE_DIM // 4))
    ang = jnp.concatenate([row[:, None] * inv_freq, col[:, None] * inv_freq], axis=-1)
    cos, sin = jnp.cos(ang), jnp.sin(ang)
    c32 = jnp.concatenate([cos, cos], axis=-1)
    s32 = jnp.concatenate([-sin, sin], axis=-1)
    c32 = jnp.concatenate([c32, jnp.ones((ctx_len, ROPE_DIM), F32)], axis=0)
    s32 = jnp.concatenate([s32, jnp.zeros((ctx_len, ROPE_DIM), F32)], axis=0)
    return jnp.concatenate([jnp.tile(c32, (1, MLA_HEADS)), jnp.tile(s32, (1, MLA_HEADS))], axis=-1)


def _swap_halves(w):
    lead = w.shape[:-1]
    w4 = w.reshape(lead + (-1, 2, ROPE_DIM // 2))
    return w4[..., ::-1, :].reshape(w.shape)


def kernel(x, c, ctx, c_ctx, w_mod, b_mod, norm1_g, norm2_g, w_in, q_norm_g, w_qb, kv_norm_g, w_kvb,
           conv_w, conv_b, a_log, dt_bias, d_skip, ssd_norm_g, w_out, router_w1, router_b1,
           router_w2, router_b2, w_gate, w_up, w_down, final_g):
    b, seq, d = x.shape
    ctx_len = ctx.shape[1]
    depth = w_mod.shape[0]
    assert ctx_len == TOKEN_TILE and seq % TOKEN_TILE == 0 and seq % GRID_W == 0
    n_lat_tiles = seq // TOKEN_TILE
    n_lat_chunks = seq // CHUNK

    xc = jnp.concatenate([x, ctx], axis=1)
    tab = _rope_tables(seq, ctx_len)

    cvec = jnp.zeros((SUBLANES, d), F32).at[:b].set(c).at[b].set(c_ctx)
    mod = _modulation(cvec, w_mod, b_mod).reshape(depth, SUBLANES, 6, d)

    for l in range(depth):
        last_layer = l == depth - 1
        ml = mod[l]
        modsel = jnp.stack([ml[:b], jnp.broadcast_to(ml[b], (b, 6, d))], axis=1)
        modsel = jnp.pad(modsel, ((0, 0), (0, 0), (0, SUBLANES - 6), (0, 0)))

        w = w_in[l]
        o_kva, o_kr = Q_LORA, Q_LORA + KV_LORA
        o_z = o_kr + ROPE_DIM
        o_xbc, o_dt = o_z + SSD_INNER, o_z + SSD_INNER + XBC_DIM
        w_kr = w[:, o_kr:o_z]
        win = jnp.concatenate([w[:, :o_kr], w[:, o_z:o_dt], w[:, o_dt:], w_kr, _swap_halves(w_kr),
                               jnp.zeros((d, IN_COLS_PAD - COL_MISC - MISC_KRP - ROPE_DIM), F32)],
                              axis=-1).astype(BF16)
        wq3 = w_qb[l].reshape(Q_LORA, MLA_HEADS, NOPE_DIM + ROPE_DIM)
        wq_rope = wq3[:, :, NOPE_DIM:].reshape(Q_LORA, MLA_HEADS * ROPE_DIM)
        wq = jnp.concatenate([wq3[:, :, :NOPE_DIM].reshape(Q_LORA, MLA_HEADS * NOPE_DIM),
                              wq_rope, _swap_halves(wq_rope)], axis=-1).astype(BF16)
        wkv3 = w_kvb[l].reshape(KV_LORA, MLA_HEADS, NOPE_DIM + V_DIM)
        wkbt = jnp.transpose(wkv3[:, :, :NOPE_DIM], (1, 2, 0)).astype(BF16)
        wvb = jnp.transpose(wkv3[:, :, NOPE_DIM:], (1, 0, 2)).astype(BF16)

        q, kv, z, xbc, misc = _inproj(xc, modsel, norm1_g[l][None], win, q_norm_g[l][None],
                                      kv_norm_g[l][None], wq, wkbt, tab, n_lat_tiles)
        att = _attention(q, kv, wvb, n_lat_tiles, ctx_len)

        cw = jnp.pad(conv_w[l], ((0, SUBLANES - CONV_W), (0, 0)))
        xbc_act = _conv(xbc, cw, conv_b[l][None], n_lat_tiles)
        bias_pad = jnp.pad(dt_bias[l].reshape(1, -1), ((0, 0), (0, LANES - 2 * SSD_HEADS)))
        alog_pad = jnp.pad(a_log[l].reshape(1, -1), ((0, 0), (0, LANES - 2 * SSD_HEADS)))
        dskip_wide = jnp.repeat(d_skip[l], SSD_HEAD_DIM)[None]
        y0 = _ssd(xbc_act, misc, bias_pad, alog_pad, dskip_wide, 0, n_lat_chunks)
        y1 = _ssd(xbc_act, misc, bias_pad, alog_pad, dskip_wide, 1, n_lat_chunks)

        rw = jnp.pad(jnp.concatenate([router_w2[l], router_w1[l]], axis=-1),
                     ((0, 0), (0, LANES - N_EXPERTS - N_GROUPS)))
        rb = jnp.pad(jnp.concatenate([router_b2[l], router_b1[l]])[None],
                     ((0, 0), (0, LANES - N_EXPERTS - N_GROUPS)))
        xn, h2, gate = _mixout(att, y0, y1, z, xc, modsel, ssd_norm_g[l][None], w_out[l].astype(BF16),
                               norm2_g[l][None], rw, rb, n_lat_tiles)
        xc = _moe(h2, gate, xn, modsel, w_gate[l].astype(BF16), w_up[l].astype(BF16),
                  w_down[l].astype(BF16), final_g[None], seq, last_layer)
    return xc
```

```python
import functools

import jax
import jax.numpy as jnp
from jax import lax
from jax.experimental import pallas as pl
from jax.experimental.pallas import tpu as pltpu

F32 = jnp.float32
BF16 = jnp.bfloat16
HIGHEST = lax.Precision.HIGHEST

GRID_W = 64
MLA_HEADS = 8
Q_LORA = 256
KV_LORA = 128
NOPE_DIM = 64
ROPE_DIM = 32
V_DIM = 64
ROPE_THETA = 10000.0
SSD_HEADS = 8
SSD_HEAD_DIM = 64
SSD_INNER = SSD_HEADS * SSD_HEAD_DIM
SSD_GROUPS = 2
SSD_STATE = 64
CONV_W = 5
CHUNK = 128
XBC_DIM = SSD_INNER + 2 * SSD_GROUPS * SSD_STATE
N_GROUPS = 4
EXPERTS_PER_GROUP = 8
N_EXPERTS = N_GROUPS * EXPERTS_PER_GROUP
EXPERT_FF = 256
EPS = 1e-6

LANES = 128
SUBLANES = 8
TOKEN_TILE = 256
KQ_DIM = KV_LORA + ROPE_DIM
ATT_SCALE = (NOPE_DIM + ROPE_DIM) ** -0.5
NEG_BIG = -1e30

COL_QA = 0
COL_KVA = COL_QA + Q_LORA
COL_Z = COL_KVA + KV_LORA
COL_XBC = COL_Z + SSD_INNER
COL_MISC = COL_XBC + XBC_DIM
COL_KR = COL_MISC + LANES
COL_KRP = COL_KR + LANES
IN_COLS_PAD = COL_KRP + LANES
KQ_PAD = 2 * LANES
ONE_LANE = ROPE_DIM
LOG2E = 1.4426950408889634
ATT_CHUNK = 256
ATT_ROW_SPLIT = 2


def _params(sem, vmem_mb=48):
    return pltpu.CompilerParams(dimension_semantics=sem, vmem_limit_bytes=vmem_mb * 1024 * 1024)


def _pick(n, cands):
    for c in cands:
        if n % c == 0:
            return c
    raise ValueError(f"no tile for {n} in {cands}")


def _rms(x, g):
    return x * lax.rsqrt(jnp.mean(x * x, axis=-1, keepdims=True) + EPS) * g


def _silu(x):
    return x * (1.0 / (1.0 + jnp.exp(-x)))


def _bdot(a, b):
    return jnp.dot(a.astype(BF16), b.astype(BF16), preferred_element_type=F32)


def _mod_kernel(c_ref, w_ref, b_ref, o_ref):
    s = _silu(c_ref[...])
    o_ref[0] = jnp.dot(s, w_ref[0], precision=HIGHEST, preferred_element_type=F32) + b_ref[0]


def _modulation(cvec, w_mod, b_mod):
    depth, d, n = w_mod.shape
    tn = _pick(n, (768, 512, 256, 128))
    return pl.pallas_call(
        _mod_kernel,
        grid=(depth, n // tn),
        in_specs=[pl.BlockSpec((SUBLANES, d), lambda l, j: (0, 0)),
                  pl.BlockSpec((1, d, tn), lambda l, j: (l, 0, j)),
                  pl.BlockSpec((1, 1, tn), lambda l, j: (l, 0, j))],
        out_specs=pl.BlockSpec((1, SUBLANES, tn), lambda l, j: (l, 0, j)),
        out_shape=jax.ShapeDtypeStruct((depth, SUBLANES, n), F32),
        compiler_params=_params(("parallel", "parallel")),
        name="modulation",
    )(cvec, w_mod, b_mod.reshape(depth, 1, n))


def _inproj_kernel(x_ref, mod_ref, g_ref, win_ref, qg_ref, kvg_ref, wq_ref, wkbt_ref, tab_ref,
                   q_ref, kv_ref, z_ref, xbc_ref, misc_ref):
    x = x_ref[0]
    shift = mod_ref[0, 0, 0:1, :]
    scale = mod_ref[0, 0, 1:2, :]
    h = _rms(x, g_ref[...]) * (1.0 + scale) + shift
    p = _bdot(h, win_ref[...])
    z_ref[0] = p[:, COL_Z:COL_XBC]
    xbc_ref[0] = p[:, COL_XBC:COL_MISC]
    misc_ref[0] = p[:, COL_MISC:COL_KR]

    rope_c = tab_ref[:, :LANES]
    rope_s = tab_ref[:, LANES:]
    lane = lax.broadcasted_iota(jnp.int32, (1, LANES), 1)
    one_col = jnp.where(lane == ONE_LANE, 1.0, 0.0)

    ckv = _rms(p[:, COL_KVA:COL_Z], kvg_ref[...])
    k_rope = p[:, COL_KR:COL_KRP] * rope_c + p[:, COL_KRP:IN_COLS_PAD] * rope_s + one_col
    kv_ref[0] = jnp.concatenate([ckv, k_rope], axis=-1).astype(BF16)

    qn = _rms(p[:, COL_QA:COL_KVA], qg_ref[...])
    q3 = _bdot(qn, wq_ref[...])
    n_nope = MLA_HEADS * NOPE_DIM
    n_grp = MLA_HEADS * LANES
    q_scale = ATT_SCALE * LOG2E
    q_nope = (q3[:, :n_nope] * q_scale).astype(BF16)
    for hd in range(MLA_HEADS):
        q_abs = jnp.dot(q_nope[:, hd * NOPE_DIM:(hd + 1) * NOPE_DIM], wkbt_ref[hd],
                        preferred_element_type=F32)
        lo = n_nope + hd * LANES
        q_rope = (q3[:, lo:lo + LANES] * rope_c + q3[:, lo + n_grp:lo + n_grp + LANES] * rope_s) * q_scale
        q_ref[0, hd] = jnp.concatenate([q_abs, q_rope], axis=-1).astype(BF16)


def _inproj(xc, modsel, g1, win, qg, kvg, wq, wkbt, tab, n_lat_tiles):
    b, lc, d = xc.shape
    tm = TOKEN_TILE
    nt = lc // tm
    const2 = lambda bi, i: (0, 0)
    return pl.pallas_call(
        _inproj_kernel,
        grid=(b, nt),
        in_specs=[pl.BlockSpec((1, tm, d), lambda bi, i: (bi, i, 0)),
                  pl.BlockSpec((1, 1, SUBLANES, d), lambda bi, i: (bi, i // n_lat_tiles, 0, 0)),
                  pl.BlockSpec((1, d), const2),
                  pl.BlockSpec((d, IN_COLS_PAD), const2),
                  pl.BlockSpec((1, Q_LORA), const2),
                  pl.BlockSpec((1, KV_LORA), const2),
                  pl.BlockSpec(wq.shape, const2),
                  pl.BlockSpec(wkbt.shape, lambda bi, i: (0, 0, 0)),
                  pl.BlockSpec((tm, tab.shape[1]), lambda bi, i: (i, 0))],
        out_specs=[pl.BlockSpec((1, MLA_HEADS, tm, KQ_PAD), lambda bi, i: (bi, 0, i, 0)),
                   pl.BlockSpec((1, tm, KQ_PAD), lambda bi, i: (bi, i, 0)),
                   pl.BlockSpec((1, tm, SSD_INNER), lambda bi, i: (bi, i, 0)),
                   pl.BlockSpec((1, tm, XBC_DIM), lambda bi, i: (bi, i, 0)),
                   pl.BlockSpec((1, tm, LANES), lambda bi, i: (bi, i, 0))],
        out_shape=[jax.ShapeDtypeStruct((b, MLA_HEADS, lc, KQ_PAD), BF16),
                   jax.ShapeDtypeStruct((b, lc, KQ_PAD), BF16),
                   jax.ShapeDtypeStruct((b, lc, SSD_INNER), F32),
                   jax.ShapeDtypeStruct((b, lc, XBC_DIM), F32),
                   jax.ShapeDtypeStruct((b, lc, LANES), F32)],
        compiler_params=_params(("parallel", "parallel")),
        name="inproj",
    )(xc, modsel, g1, win, qg, kvg, wq, wkbt, tab)


def _attn_kernel(q_ref, kv_ref, wvb_ref, o_ref, m_sc, acc_sc, *, tq, tk, tc, row_split, nk, n_lat_q, ctx_len):
    qi = pl.program_id(1)
    kj = pl.program_id(2)
    rows = MLA_HEADS * tq

    @pl.when(kj == 0)
    def _():
        m_sc[...] = jnp.full(m_sc.shape, NEG_BIG, F32)
        acc_sc[...] = jnp.zeros(acc_sc.shape, F32)

    def chunk(q, r0, lo, width):
        nr = q.shape[0]
        kvc = kv_ref[0, lo:lo + width, :]
        s = lax.dot_general(q, kvc, (((1,), (1,)), ((), ())), preferred_element_type=F32)
        m_prev = m_sc[r0:r0 + nr, :]
        m_new = jnp.maximum(m_prev, jnp.max(s, axis=-1, keepdims=True))
        alpha = jnp.exp2(m_prev - m_new)
        p = jnp.exp2(s - pltpu.repeat(m_new, width // LANES, axis=1))
        acc_sc[r0:r0 + nr, :] = (pltpu.repeat(alpha, KQ_PAD // LANES, axis=1) * acc_sc[r0:r0 + nr, :]
                                 + jnp.dot(p.astype(BF16), kvc, preferred_element_type=F32))
        m_sc[r0:r0 + nr, :] = m_new

    def step(ctx_only):
        nr = rows // row_split
        qs = [q_ref[0, g * (MLA_HEADS // row_split):(g + 1) * (MLA_HEADS // row_split)].reshape(nr, KQ_PAD)
              for g in range(row_split)]
        spans = [(tk - ctx_len, ctx_len)] if ctx_only else [(c * tc, tc) for c in range(tk // tc)]
        for lo, width in spans:
            for g in range(row_split):
                chunk(qs[g], g * nr, lo, width)

    is_ctx_q = qi >= n_lat_q
    pl.when(jnp.logical_not(is_ctx_q))(functools.partial(step, False))
    pl.when(jnp.logical_and(is_ctx_q, kj == nk - 1))(functools.partial(step, True))

    @pl.when(kj == nk - 1)
    def _():
        acc = acc_sc[...]
        denom = acc[:, LANES + ONE_LANE:LANES + ONE_LANE + 1]
        o = (acc[:, :KV_LORA] / denom).astype(BF16)
        outs = [jnp.dot(o[hd * tq:(hd + 1) * tq], wvb_ref[hd], preferred_element_type=F32)
                for hd in range(MLA_HEADS)]
        o_ref[0] = jnp.concatenate(outs, axis=-1).astype(BF16)


def _attention(q, kv, wvb, n_lat_q, ctx_len):
    b, _, lc, _ = q.shape
    tq = TOKEN_TILE
    tk = _pick(lc, (3328, 1280, 768, 512, 256))
    tc = ATT_CHUNK
    assert ctx_len <= tk and ctx_len == TOKEN_TILE
    nq, nk = lc // tq, lc // tk
    rows = MLA_HEADS * tq
    kern = functools.partial(_attn_kernel, tq=tq, tk=tk, tc=tc, row_split=ATT_ROW_SPLIT, nk=nk, n_lat_q=n_lat_q, ctx_len=ctx_len)
    return pl.pallas_call(
        kern,
        grid=(b, nq, nk),
        in_specs=[pl.BlockSpec((1, MLA_HEADS, tq, KQ_PAD), lambda bi, i, j: (bi, 0, i, 0)),
                  pl.BlockSpec((1, tk, KQ_PAD), lambda bi, i, j: (bi, j, 0)),
                  pl.BlockSpec(wvb.shape, lambda bi, i, j: (0, 0, 0))],
        out_specs=pl.BlockSpec((1, tq, MLA_HEADS * V_DIM), lambda bi, i, j: (bi, i, 0)),
        out_shape=jax.ShapeDtypeStruct((b, lc, MLA_HEADS * V_DIM), BF16),
        scratch_shapes=[pltpu.VMEM((rows, LANES), F32), pltpu.VMEM((rows, KQ_PAD), F32)],
        compiler_params=_params(("parallel", "parallel", "arbitrary")),
        name="attention",
    )(q, kv, wvb)


def _conv_kernel(x_ref, prev_ref, next_ref, w_ref, b_ref, o_ref, ext_sc, *, tm, n_lat_tiles):
    i = pl.program_id(1)
    halo = SUBLANES
    prev_ok = jnp.logical_and(i >= 1, i < n_lat_tiles)
    next_ok = i < n_lat_tiles - 1
    ext_sc[0:halo, :] = jnp.where(prev_ok, prev_ref[0], 0.0)
    ext_sc[halo:halo + tm, :] = x_ref[0]
    ext_sc[halo + tm:2 * halo + tm, :] = jnp.where(next_ok, next_ref[0], 0.0)
    acc = jnp.broadcast_to(b_ref[...], (tm, XBC_DIM))
    for k in range(CONV_W):
        off = halo - CONV_W // 2 + k
        acc = acc + w_ref[k:k + 1, :] * ext_sc[off:off + tm, :]
    o_ref[0] = _silu(acc)


def _conv(xbc, conv_w, conv_b, n_lat_tiles):
    b, lc, c = xbc.shape
    tm = TOKEN_TILE
    nt = lc // tm
    hb = tm // SUBLANES
    last = lc // SUBLANES - 1
    kern = functools.partial(_conv_kernel, tm=tm, n_lat_tiles=n_lat_tiles)
    return pl.pallas_call(
        kern,
        grid=(b, nt),
        in_specs=[pl.BlockSpec((1, tm, c), lambda bi, i: (bi, i, 0)),
                  pl.BlockSpec((1, SUBLANES, c), lambda bi, i: (bi, jnp.maximum(i * hb - 1, 0), 0)),
                  pl.BlockSpec((1, SUBLANES, c), lambda bi, i: (bi, jnp.minimum((i + 1) * hb, last), 0)),
                  pl.BlockSpec((SUBLANES, c), lambda bi, i: (0, 0)),
                  pl.BlockSpec((1, c), lambda bi, i: (0, 0))],
        out_specs=pl.BlockSpec((1, tm, c), lambda bi, i: (bi, i, 0)),
        out_shape=jax.ShapeDtypeStruct((b, lc, c), F32),
        scratch_shapes=[pltpu.VMEM((tm + 2 * SUBLANES, c), F32)],
        compiler_params=_params(("parallel", "parallel")),
        name="conv",
    )(xbc, xbc, xbc, conv_w, conv_b)


def _ssd_kernel(xbc_ref, misc_ref, bias_ref, alog_ref, alogt_ref, dskip_ref, y_ref, h_sc, *, direction):
    i = pl.program_id(1)
    q = CHUNK
    n = SSD_STATE
    pdim = SSD_HEAD_DIM

    @pl.when(i == 0)
    def _():
        h_sc[...] = jnp.zeros(h_sc.shape, F32)

    xbc = xbc_ref[0]
    xs = xbc[:, :SSD_INNER]
    bmat = xbc[:, SSD_INNER:SSD_INNER + SSD_GROUPS * n]
    cmat = xbc[:, SSD_INNER + SSD_GROUPS * n:]
    raw = misc_ref[0] + bias_ref[...]
    softplus = lambda v: jnp.maximum(v, 0.0) + jnp.log(1.0 + jnp.exp(-jnp.abs(v)))
    nd = 2 * SSD_HEADS
    lane = lax.broadcasted_iota(jnp.int32, (1, LANES), 1)
    a = softplus(raw) * jnp.where(lane < nd, -jnp.exp(alog_ref[...]), 0.0)
    dt_t = softplus(raw.T[0:nd, :])
    a_t = dt_t * (-jnp.exp(alogt_ref[...]))

    r_i = lax.broadcasted_iota(jnp.int32, (q, q), 0)
    c_i = lax.broadcasted_iota(jnp.int32, (q, q), 1)
    valid = (c_i <= r_i) if direction == 0 else (c_i >= r_i)
    tri = jnp.where(valid, 1.0, 0.0).astype(F32)
    tri_t = jnp.where(valid, 0.0, 1.0).astype(F32) + jnp.where(r_i == c_i, 1.0, 0.0).astype(F32)
    acs = jnp.dot(tri, a, precision=HIGHEST, preferred_element_type=F32)
    acs_t = jnp.dot(a_t, tri_t, precision=HIGHEST, preferred_element_type=F32)
    bt = bmat.T
    last = q - 1 if direction == 0 else 0
    xb = xs.astype(BF16)

    cb = [jnp.dot(cmat[:, g * n:(g + 1) * n].astype(BF16), bt[g * n:(g + 1) * n, :].astype(BF16),
                  preferred_element_type=F32) for g in range(SSD_GROUPS)]
    ys = []
    for hd in range(SSD_HEADS):
        j = direction * SSD_HEADS + hd
        g = hd // (SSD_HEADS // SSD_GROUPS)
        col = jnp.broadcast_to(acs[:, j:j + 1], (q, q))
        row = acs_t[j:j + 1, :]
        dt_row = dt_t[j:j + 1, :]
        atot = acs_t[j:j + 1, last:last + 1]
        decay = jnp.exp(jnp.where(valid, col - row, NEG_BIG))
        xh = xb[:, hd * pdim:(hd + 1) * pdim]
        h_prev = h_sc[hd]
        c_g = cmat[:, g * n:(g + 1) * n]
        y = (jnp.dot((cb[g] * decay * dt_row).astype(BF16), xh, preferred_element_type=F32)
             + jnp.dot((c_g * jnp.exp(col[:, :n])).astype(BF16), h_prev.astype(BF16),
                       preferred_element_type=F32))
        if direction == 0:
            y = y + dskip_ref[:, hd * pdim:(hd + 1) * pdim] * xs[:, hd * pdim:(hd + 1) * pdim]
        ys.append(y)
        bw = (bt[g * n:(g + 1) * n, :] * (jnp.exp(atot - row) * dt_row)).astype(BF16)
        h_sc[hd] = jnp.exp(atot) * h_prev + jnp.dot(bw, xh, preferred_element_type=F32)
    y_ref[0] = jnp.concatenate(ys, axis=-1)


def _ssd(xbc_act, misc, bias_pad, alog_pad, dskip_wide, direction, n_lat_chunks):
    b, lc, c = xbc_act.shape
    alog_t = jnp.broadcast_to(alog_pad[0, :2 * SSD_HEADS, None], (2 * SSD_HEADS, LANES))
    nc = lc // CHUNK
    if direction == 0:
        cmap = lambda bi, i: (bi, (i + n_lat_chunks) % nc, 0)
    else:
        cmap = lambda bi, i: (bi, nc - 1 - i, 0)
    const2 = lambda bi, i: (0, 0)
    kern = functools.partial(_ssd_kernel, direction=direction)
    return pl.pallas_call(
        kern,
        grid=(b, nc),
        in_specs=[pl.BlockSpec((1, CHUNK, c), cmap),
                  pl.BlockSpec((1, CHUNK, LANES), cmap),
                  pl.BlockSpec((1, LANES), const2),
                  pl.BlockSpec((1, LANES), const2),
                  pl.BlockSpec((2 * SSD_HEADS, LANES), const2),
                  pl.BlockSpec((1, SSD_INNER), const2)],
        out_specs=pl.BlockSpec((1, CHUNK, SSD_INNER), cmap),
        out_shape=jax.ShapeDtypeStruct((b, lc, SSD_INNER), F32),
        scratch_shapes=[pltpu.VMEM((SSD_HEADS, SSD_STATE, SSD_HEAD_DIM), F32)],
        compiler_params=_params(("parallel", "arbitrary")),
        name=f"ssd_dir{direction}",
    )(xbc_act, misc, bias_pad, alog_pad, alog_t, dskip_wide)


def _mixout_kernel(att_ref, y0_ref, y1_ref, z_ref, x_ref, mod_ref, sg_ref, wout_ref, g2_ref,
                   rwh_ref, rwl_ref, rb_ref, xn_ref, h2_ref, gate_ref):
    y = (y0_ref[0] + y1_ref[0]) * _silu(z_ref[0])
    ssd = _rms(y, sg_ref[...])
    half = MLA_HEADS * V_DIM
    o = (jnp.dot(att_ref[0], wout_ref[0:half, :], preferred_element_type=F32)
         + jnp.dot(ssd.astype(BF16), wout_ref[half:, :], preferred_element_type=F32))
    xn = x_ref[0] + mod_ref[0, 0, 2:3, :] * o
    xn_ref[0] = xn
    h2 = _rms(xn, g2_ref[...]) * (1.0 + mod_ref[0, 0, 4:5, :]) + mod_ref[0, 0, 3:4, :]
    h2_ref[0] = h2.astype(BF16)

    h2_hi = h2.astype(BF16)
    h2_lo = (h2 - h2_hi.astype(F32)).astype(BF16)
    logits = (jnp.dot(h2_hi, rwh_ref[...], preferred_element_type=F32)
              + jnp.dot(h2_lo, rwh_ref[...], preferred_element_type=F32)
              + jnp.dot(h2_hi, rwl_ref[...], preferred_element_type=F32)) + rb_ref[...]
    lane = lax.broadcasted_iota(jnp.int32, logits.shape, 1)
    is_grp = jnp.logical_and(lane >= N_EXPERTS, lane < N_EXPERTS + N_GROUPS)
    l1 = jnp.where(is_grp, logits, NEG_BIG)
    m1 = jnp.max(l1, axis=-1, keepdims=True)
    p_top = 1.0 / jnp.sum(jnp.exp(l1 - m1), axis=-1, keepdims=True)
    g_idx = jnp.min(jnp.where(jnp.logical_and(is_grp, l1 == m1), lane - N_EXPERTS, N_GROUPS),
                    axis=-1, keepdims=True)
    sel = jnp.logical_and(lane < N_EXPERTS, (lane // EXPERTS_PER_GROUP) == g_idx)
    l2 = jnp.where(sel, logits, NEG_BIG)
    m2 = jnp.max(l2, axis=-1, keepdims=True)
    i_first = jnp.min(jnp.where(jnp.logical_and(sel, l2 == m2), lane, LANES), axis=-1, keepdims=True)
    l3 = jnp.where(lane == i_first, NEG_BIG, l2)
    m3 = jnp.max(l3, axis=-1, keepdims=True)
    i_second = jnp.min(jnp.where(jnp.logical_and(sel, l3 == m3), lane, LANES), axis=-1, keepdims=True)
    e2 = jnp.exp(m3 - m2)
    inv = p_top / (1.0 + e2)
    gate_ref[0] = jnp.where(lane == i_first, inv, jnp.where(lane == i_second, e2 * inv, 0.0))


def _mixout(att, y0, y1, z, xc, modsel, sg, wout, g2, rw, rb, n_lat_tiles):
    rw_hi = rw.astype(BF16)
    rw_lo = (rw - rw_hi.astype(F32)).astype(BF16)
    b, lc, d = xc.shape
    tm = TOKEN_TILE
    nt = lc // tm
    tok = lambda w: pl.BlockSpec((1, tm, w), lambda bi, i: (bi, i, 0))
    const2 = lambda bi, i: (0, 0)
    return pl.pallas_call(
        _mixout_kernel,
        grid=(b, nt),
        in_specs=[tok(SSD_INNER), tok(SSD_INNER), tok(SSD_INNER), tok(SSD_INNER), tok(d),
                  pl.BlockSpec((1, 1, SUBLANES, d), lambda bi, i: (bi, i // n_lat_tiles, 0, 0)),
                  pl.BlockSpec((1, SSD_INNER), const2),
                  pl.BlockSpec(wout.shape, const2),
                  pl.BlockSpec((1, d), const2),
                  pl.BlockSpec((d, LANES), const2),
                  pl.BlockSpec((d, LANES), const2),
                  pl.BlockSpec((1, LANES), const2)],
        out_specs=[tok(d), tok(d), tok(LANES)],
        out_shape=[jax.ShapeDtypeStruct((b, lc, d), F32),
                   jax.ShapeDtypeStruct((b, lc, d), BF16),
                   jax.ShapeDtypeStruct((b, lc, LANES), F32)],
        compiler_params=_params(("parallel", "parallel")),
        name="mixout",
    )(att, y0, y1, z, xc, modsel, sg, wout, g2, rw_hi, rw_lo, rb)


def _moe_kernel(h2_ref, gate_ref, xn_ref, mod_ref, wg_ref, wu_ref, wd_ref, fg_ref, o_ref, acc_sc,
                *, tmm, seq, final):
    t = pl.program_id(1)
    e = pl.program_id(2)

    @pl.when(e == 0)
    def _():
        acc_sc[...] = jnp.zeros(acc_sc.shape, F32)

    h = h2_ref[0]
    gate = gate_ref[0]
    lane = lax.broadcasted_iota(jnp.int32, gate.shape, 1)
    gcol = jnp.sum(jnp.where(lane == e, gate, 0.0), axis=-1, keepdims=True)
    hg = jnp.dot(h, wg_ref[0], preferred_element_type=F32)
    hu = jnp.dot(h, wu_ref[0], preferred_element_type=F32)
    act = (_silu(hg) * hu * gcol).astype(BF16)
    acc_sc[...] += jnp.dot(act, wd_ref[0], preferred_element_type=F32)

    @pl.when(e == N_EXPERTS - 1)
    def _():
        row = t * tmm + lax.broadcasted_iota(jnp.int32, (tmm, 1), 0)
        g2 = jnp.where(row >= seq, mod_ref[0, 1, 5:6, :], mod_ref[0, 0, 5:6, :])
        out = xn_ref[0] + g2 * acc_sc[...]
        if final:
            out = _rms(out, fg_ref[...])
        o_ref[0] = out


def _moe(h2, gate, xn, modsel, wg, wu, wd, fg, seq, final):
    b, lc, d = xn.shape
    if final:
        rows_out = seq
        tmm = _pick(seq, (1024, 512, 256))
    else:
        rows_out = lc
        tmm = _pick(lc, (1280, 768, 512, 256))
    nt = rows_out // tmm
    kern = functools.partial(_moe_kernel, tmm=tmm, seq=seq, final=final)
    tok = lambda w: pl.BlockSpec((1, tmm, w), lambda bi, t, e: (bi, t, 0))
    return pl.pallas_call(
        kern,
        grid=(b, nt, N_EXPERTS),
        in_specs=[tok(d), tok(LANES), tok(d),
                  pl.BlockSpec((1, 2, SUBLANES, d), lambda bi, t, e: (bi, 0, 0, 0)),
                  pl.BlockSpec((1, d, EXPERT_FF), lambda bi, t, e: (e, 0, 0)),
                  pl.BlockSpec((1, d, EXPERT_FF), lambda bi, t, e: (e, 0, 0)),
                  pl.BlockSpec((1, EXPERT_FF, d), lambda bi, t, e: (e, 0, 0)),
                  pl.BlockSpec((1, d), lambda bi, t, e: (0, 0))],
        out_specs=tok(d),
        out_shape=jax.ShapeDtypeStruct((b, rows_out, d), F32),
        scratch_shapes=[pltpu.VMEM((tmm, d), F32)],
        compiler_params=_params(("parallel", "parallel", "arbitrary"), vmem_mb=56),
        name="moe_final" if final else "moe",
    )(h2, gate, xn, modsel, wg, wu, wd, fg)


def _rope_tables(seq, ctx_len):
    rows = seq // GRID_W
    row = jnp.repeat(jnp.arange(rows), GRID_W)
    col = jnp.tile(jnp.arange(GRID_W), rows)
    inv_freq = ROPE_THETA ** (-jnp.arange(ROPE_DIM // 4, dtype=F32) / (ROPE_DIM // 4))
    ang = jnp.concatenate([row[:, None] * inv_freq, col[:, None] * inv_freq], axis=-1)
    cos, sin = jnp.cos(ang), jnp.sin(ang)
    c32 = jnp.concatenate([cos, cos], axis=-1)
    s32 = jnp.concatenate([-sin, sin], axis=-1)
    c32 = jnp.concatenate([c32, jnp.ones((ctx_len, ROPE_DIM), F32)], axis=0)
    s32 = jnp.concatenate([s32, jnp.zeros((ctx_len, ROPE_DIM), F32)], axis=0)
    pad = ((0, 0), (0, LANES - ROPE_DIM))
    return jnp.concatenate([jnp.pad(c32, pad), jnp.pad(s32, pad)], axis=-1)


def _swap_halves(w):
    lead = w.shape[:-1]
    w4 = w.reshape(lead + (-1, 2, ROPE_DIM // 2))
    return w4[..., ::-1, :].reshape(w.shape)


def kernel(x, c, ctx, c_ctx, w_mod, b_mod, norm1_g, norm2_g, w_in, q_norm_g, w_qb, kv_norm_g, w_kvb,
           conv_w, conv_b, a_log, dt_bias, d_skip, ssd_norm_g, w_out, router_w1, router_b1,
           router_w2, router_b2, w_gate, w_up, w_down, final_g):
    b, seq, d = x.shape
    ctx_len = ctx.shape[1]
    depth = w_mod.shape[0]
    assert ctx_len == TOKEN_TILE and seq % TOKEN_TILE == 0 and seq % GRID_W == 0
    n_lat_tiles = seq // TOKEN_TILE
    n_lat_chunks = seq // CHUNK

    xc = jnp.concatenate([x, ctx], axis=1)
    tab = _rope_tables(seq, ctx_len)

    cvec = jnp.zeros((SUBLANES, d), F32).at[:b].set(c).at[b].set(c_ctx)
    mod = _modulation(cvec, w_mod, b_mod).reshape(depth, SUBLANES, 6, d)

    for l in range(depth):
        last_layer = l == depth - 1
        ml = mod[l]
        modsel = jnp.stack([ml[:b], jnp.broadcast_to(ml[b], (b, 6, d))], axis=1)
        modsel = jnp.pad(modsel, ((0, 0), (0, 0), (0, SUBLANES - 6), (0, 0)))

        w = w_in[l]
        o_kva, o_kr = Q_LORA, Q_LORA + KV_LORA
        o_z = o_kr + ROPE_DIM
        o_xbc, o_dt = o_z + SSD_INNER, o_z + SSD_INNER + XBC_DIM
        w_kr = w[:, o_kr:o_z]
        lane_pad = lambda t: jnp.pad(t, [(0, 0)] * (t.ndim - 1) + [(0, LANES - t.shape[-1])])
        win = jnp.concatenate([w[:, :o_kr], w[:, o_z:o_dt], lane_pad(w[:, o_dt:]), lane_pad(w_kr),
                               lane_pad(_swap_halves(w_kr))], axis=-1).astype(BF16)
        wq3 = w_qb[l].reshape(Q_LORA, MLA_HEADS, NOPE_DIM + ROPE_DIM)
        wq_rope = wq3[:, :, NOPE_DIM:]
        wq = jnp.concatenate([wq3[:, :, :NOPE_DIM].reshape(Q_LORA, MLA_HEADS * NOPE_DIM),
                              lane_pad(wq_rope).reshape(Q_LORA, MLA_HEADS * LANES),
                              lane_pad(_swap_halves(wq_rope)).reshape(Q_LORA, MLA_HEADS * LANES)],
                             axis=-1).astype(BF16)
        wkv3 = w_kvb[l].reshape(KV_LORA, MLA_HEADS, NOPE_DIM + V_DIM)
        wkbt = jnp.transpose(wkv3[:, :, :NOPE_DIM], (1, 2, 0)).astype(BF16)
        wvb = jnp.transpose(wkv3[:, :, NOPE_DIM:], (1, 0, 2)).astype(BF16)

        q, kv, z, xbc, misc = _inproj(xc, modsel, norm1_g[l][None], win, q_norm_g[l][None],
                                      kv_norm_g[l][None], wq, wkbt, tab, n_lat_tiles)
        att = _attention(q, kv, wvb, n_lat_tiles, ctx_len)

        cw = jnp.pad(conv_w[l], ((0, SUBLANES - CONV_W), (0, 0)))
        xbc_act = _conv(xbc, cw, conv_b[l][None], n_lat_tiles)
        bias_pad = jnp.pad(dt_bias[l].reshape(1, -1), ((0, 0), (0, LANES - 2 * SSD_HEADS)))
        alog_pad = jnp.pad(a_log[l].reshape(1, -1), ((0, 0), (0, LANES - 2 * SSD_HEADS)))
        dskip_wide = jnp.repeat(d_skip[l], SSD_HEAD_DIM)[None]
        y0 = _ssd(xbc_act, misc, bias_pad, alog_pad, dskip_wide, 0, n_lat_chunks)
        y1 = _ssd(xbc_act, misc, bias_pad, alog_pad, dskip_wide, 1, n_lat_chunks)

        rw = jnp.pad(jnp.concatenate([router_w2[l], router_w1[l]], axis=-1),
                     ((0, 0), (0, LANES - N_EXPERTS - N_GROUPS)))
        rb = jnp.pad(jnp.concatenate([router_b2[l], router_b1[l]])[None],
                     ((0, 0), (0, LANES - N_EXPERTS - N_GROUPS)))
        xn, h2, gate = _mixout(att, y0, y1, z, xc, modsel, ssd_norm_g[l][None], w_out[l].astype(BF16),
                               norm2_g[l][None], rw, rb, n_lat_tiles)
        xc = _moe(h2, gate, xn, modsel, w_gate[l].astype(BF16), w_up[l].astype(BF16),
                  w_down[l].astype(BF16), final_g[None], seq, last_layer)
    return xc
```

```python
import functools

import jax
import jax.numpy as jnp
from jax import lax
from jax.experimental import pallas as pl
from jax.experimental.pallas import tpu as pltpu

F32 = jnp.float32
BF16 = jnp.bfloat16
HIGHEST = lax.Precision.HIGHEST

GRID_W = 64
MLA_HEADS = 8
Q_LORA = 256
KV_LORA = 128
NOPE_DIM = 64
ROPE_DIM = 32
V_DIM = 64
ROPE_THETA = 10000.0
SSD_HEADS = 8
SSD_HEAD_DIM = 64
SSD_INNER = SSD_HEADS * SSD_HEAD_DIM
SSD_GROUPS = 2
SSD_STATE = 64
CONV_W = 5
CHUNK = 128
XBC_DIM = SSD_INNER + 2 * SSD_GROUPS * SSD_STATE
N_GROUPS = 4
EXPERTS_PER_GROUP = 8
N_EXPERTS = N_GROUPS * EXPERTS_PER_GROUP
EXPERT_FF = 256
EPS = 1e-6

LANES = 128
SUBLANES = 8
TOKEN_TILE = 256
KQ_DIM = KV_LORA + ROPE_DIM
ATT_SCALE = (NOPE_DIM + ROPE_DIM) ** -0.5
NEG_BIG = -1e30

COL_QA = 0
COL_KVA = COL_QA + Q_LORA
COL_Z = COL_KVA + KV_LORA
COL_XBC = COL_Z + SSD_INNER
COL_MISC = COL_XBC + XBC_DIM
COL_KR = COL_MISC + LANES
COL_KRP = COL_KR + LANES
IN_COLS_PAD = COL_KRP + LANES
KQ_PAD = 2 * LANES
ONE_LANE = ROPE_DIM
LOG2E = 1.4426950408889634
ATT_CHUNK = 256
ATT_ROW_SPLIT = 2
MOE_BLOCK = 128
BF16_ROWS = 16
MOE_STEP_EXPERTS = 4
GRP_LANE = N_EXPERTS


def _params(sem, vmem_mb=48):
    return pltpu.CompilerParams(dimension_semantics=sem, vmem_limit_bytes=vmem_mb * 1024 * 1024)


def _pick(n, cands):
    for c in cands:
        if n % c == 0:
            return c
    raise ValueError(f"no tile for {n} in {cands}")


def _rms(x, g):
    return x * lax.rsqrt(jnp.mean(x * x, axis=-1, keepdims=True) + EPS) * g


def _silu(x):
    return x * (1.0 / (1.0 + jnp.exp(-x)))


def _lane_tile(x, n):
    return jnp.concatenate([x] * n, axis=1)


def _bdot(a, b):
    return jnp.dot(a.astype(BF16), b.astype(BF16), preferred_element_type=F32)


def _mod_kernel(c_ref, w_ref, b_ref, o_ref):
    s = _silu(c_ref[...])
    o_ref[0] = jnp.dot(s, w_ref[0], precision=HIGHEST, preferred_element_type=F32) + b_ref[0]


def _modulation(cvec, w_mod, b_mod):
    depth, d, n = w_mod.shape
    tn = _pick(n, (768, 512, 256, 128))
    return pl.pallas_call(
        _mod_kernel,
        grid=(depth, n // tn),
        in_specs=[pl.BlockSpec((SUBLANES, d), lambda l, j: (0, 0)),
                  pl.BlockSpec((1, d, tn), lambda l, j: (l, 0, j)),
                  pl.BlockSpec((1, 1, tn), lambda l, j: (l, 0, j))],
        out_specs=pl.BlockSpec((1, SUBLANES, tn), lambda l, j: (l, 0, j)),
        out_shape=jax.ShapeDtypeStruct((depth, SUBLANES, n), F32),
        compiler_params=_params(("parallel", "parallel")),
        name="modulation",
    )(cvec, w_mod, b_mod.reshape(depth, 1, n))


def _inproj_kernel(x_ref, mod_ref, g_ref, win_ref, qg_ref, kvg_ref, wq_ref, wkbt_ref, tab_ref,
                   q_ref, kv_ref, z_ref, xbc_ref, misc_ref):
    x = x_ref[0]
    shift = mod_ref[0, 0, 0:1, :]
    scale = mod_ref[0, 0, 1:2, :]
    h = _rms(x, g_ref[...]) * (1.0 + scale) + shift
    p = _bdot(h, win_ref[...])
    z_ref[0] = p[:, COL_Z:COL_XBC]
    xbc_ref[0] = p[:, COL_XBC:COL_MISC]
    misc_ref[0] = p[:, COL_MISC:COL_KR]

    rope_c = tab_ref[:, :LANES]
    rope_s = tab_ref[:, LANES:]
    lane = lax.broadcasted_iota(jnp.int32, (1, LANES), 1)
    one_col = jnp.where(lane == ONE_LANE, 1.0, 0.0)

    ckv = _rms(p[:, COL_KVA:COL_Z], kvg_ref[...])
    k_rope = p[:, COL_KR:COL_KRP] * rope_c + p[:, COL_KRP:IN_COLS_PAD] * rope_s + one_col
    kv_ref[0] = jnp.concatenate([ckv, k_rope], axis=-1).astype(BF16)

    qn = _rms(p[:, COL_QA:COL_KVA], qg_ref[...])
    q3 = _bdot(qn, wq_ref[...])
    n_nope = MLA_HEADS * NOPE_DIM
    n_grp = MLA_HEADS * LANES
    q_scale = ATT_SCALE * LOG2E
    q_nope = (q3[:, :n_nope] * q_scale).astype(BF16)
    for hd in range(MLA_HEADS):
        q_abs = jnp.dot(q_nope[:, hd * NOPE_DIM:(hd + 1) * NOPE_DIM], wkbt_ref[hd],
                        preferred_element_type=F32)
        lo = n_nope + hd * LANES
        q_rope = (q3[:, lo:lo + LANES] * rope_c + q3[:, lo + n_grp:lo + n_grp + LANES] * rope_s) * q_scale
        q_ref[0, hd] = jnp.concatenate([q_abs, q_rope], axis=-1).astype(BF16)


def _inproj(xc, modsel, g1, win, qg, kvg, wq, wkbt, tab, n_lat_tiles):
    b, lc, d = xc.shape
    tm = TOKEN_TILE
    nt = lc // tm
    const2 = lambda bi, i: (0, 0)
    return pl.pallas_call(
        _inproj_kernel,
        grid=(b, nt),
        in_specs=[pl.BlockSpec((1, tm, d), lambda bi, i: (bi, i, 0)),
                  pl.BlockSpec((1, 1, SUBLANES, d), lambda bi, i: (bi, i // n_lat_tiles, 0, 0)),
                  pl.BlockSpec((1, d), const2),
                  pl.BlockSpec((d, IN_COLS_PAD), const2),
                  pl.BlockSpec((1, Q_LORA), const2),
                  pl.BlockSpec((1, KV_LORA), const2),
                  pl.BlockSpec(wq.shape, const2),
                  pl.BlockSpec(wkbt.shape, lambda bi, i: (0, 0, 0)),
                  pl.BlockSpec((tm, tab.shape[1]), lambda bi, i: (i, 0))],
        out_specs=[pl.BlockSpec((1, MLA_HEADS, tm, KQ_PAD), lambda bi, i: (bi, 0, i, 0)),
                   pl.BlockSpec((1, tm, KQ_PAD), lambda bi, i: (bi, i, 0)),
                   pl.BlockSpec((1, tm, SSD_INNER), lambda bi, i: (bi, i, 0)),
                   pl.BlockSpec((1, tm, XBC_DIM), lambda bi, i: (bi, i, 0)),
                   pl.BlockSpec((1, tm, LANES), lambda bi, i: (bi, i, 0))],
        out_shape=[jax.ShapeDtypeStruct((b, MLA_HEADS, lc, KQ_PAD), BF16),
                   jax.ShapeDtypeStruct((b, lc, KQ_PAD), BF16),
                   jax.ShapeDtypeStruct((b, lc, SSD_INNER), F32),
                   jax.ShapeDtypeStruct((b, lc, XBC_DIM), F32),
                   jax.ShapeDtypeStruct((b, lc, LANES), F32)],
        compiler_params=_params(("parallel", "parallel")),
        name="inproj",
    )(xc, modsel, g1, win, qg, kvg, wq, wkbt, tab)


def _attn_kernel(q_ref, kv_ref, wvb_ref, o_ref, m_sc, acc_sc, *, tq, tk, tc, row_split, nk, n_lat_q, ctx_len):
    qi = pl.program_id(1)
    kj = pl.program_id(2)
    rows = MLA_HEADS * tq

    @pl.when(kj == 0)
    def _():
        m_sc[...] = jnp.full(m_sc.shape, NEG_BIG, F32)
        acc_sc[...] = jnp.zeros(acc_sc.shape, F32)

    def chunk(q, r0, lo, width):
        nr = q.shape[0]
        kvc = kv_ref[0, lo:lo + width, :]
        s = lax.dot_general(q, kvc, (((1,), (1,)), ((), ())), preferred_element_type=F32)
        m_prev = m_sc[r0:r0 + nr, :]
        m_new = jnp.maximum(m_prev, jnp.max(s, axis=-1, keepdims=True))
        alpha = jnp.exp2(m_prev - m_new)
        p = jnp.exp2(s - _lane_tile(m_new, width // LANES))
        acc_sc[r0:r0 + nr, :] = (_lane_tile(alpha, KQ_PAD // LANES) * acc_sc[r0:r0 + nr, :]
                                 + jnp.dot(p.astype(BF16), kvc, preferred_element_type=F32))
        m_sc[r0:r0 + nr, :] = m_new

    def step(ctx_only):
        nr = rows // row_split
        qs = [q_ref[0, g * (MLA_HEADS // row_split):(g + 1) * (MLA_HEADS // row_split)].reshape(nr, KQ_PAD)
              for g in range(row_split)]
        spans = [(tk - ctx_len, ctx_len)] if ctx_only else [(c * tc, tc) for c in range(tk // tc)]
        for lo, width in spans:
            for g in range(row_split):
                chunk(qs[g], g * nr, lo, width)

    is_ctx_q = qi >= n_lat_q
    pl.when(jnp.logical_not(is_ctx_q))(functools.partial(step, False))
    pl.when(jnp.logical_and(is_ctx_q, kj == nk - 1))(functools.partial(step, True))

    @pl.when(kj == nk - 1)
    def _():
        acc = acc_sc[...]
        denom = acc[:, LANES + ONE_LANE:LANES + ONE_LANE + 1]
        o = (acc[:, :KV_LORA] / denom).astype(BF16)
        outs = [jnp.dot(o[hd * tq:(hd + 1) * tq], wvb_ref[hd], preferred_element_type=F32)
                for hd in range(MLA_HEADS)]
        o_ref[0] = jnp.concatenate(outs, axis=-1).astype(BF16)


def _attention(q, kv, wvb, n_lat_q, ctx_len):
    b, _, lc, _ = q.shape
    tq = TOKEN_TILE
    tk = _pick(lc, (3328, 1280, 768, 512, 256))
    tc = ATT_CHUNK
    assert ctx_len <= tk and ctx_len == TOKEN_TILE
    nq, nk = lc // tq, lc // tk
    rows = MLA_HEADS * tq
    kern = functools.partial(_attn_kernel, tq=tq, tk=tk, tc=tc, row_split=ATT_ROW_SPLIT, nk=nk, n_lat_q=n_lat_q, ctx_len=ctx_len)
    return pl.pallas_call(
        kern,
        grid=(b, nq, nk),
        in_specs=[pl.BlockSpec((1, MLA_HEADS, tq, KQ_PAD), lambda bi, i, j: (bi, 0, i, 0)),
                  pl.BlockSpec((1, tk, KQ_PAD), lambda bi, i, j: (bi, j, 0)),
                  pl.BlockSpec(wvb.shape, lambda bi, i, j: (0, 0, 0))],
        out_specs=pl.BlockSpec((1, tq, MLA_HEADS * V_DIM), lambda bi, i, j: (bi, i, 0)),
        out_shape=jax.ShapeDtypeStruct((b, lc, MLA_HEADS * V_DIM), BF16),
        scratch_shapes=[pltpu.VMEM((rows, LANES), F32), pltpu.VMEM((rows, KQ_PAD), F32)],
        compiler_params=_params(("parallel", "parallel", "arbitrary")),
        name="attention",
    )(q, kv, wvb)


def _conv_kernel(x_ref, prev_ref, next_ref, w_ref, b_ref, o_ref, ext_sc, *, tm, n_lat_tiles):
    i = pl.program_id(1)
    halo = SUBLANES
    prev_ok = jnp.logical_and(i >= 1, i < n_lat_tiles)
    next_ok = i < n_lat_tiles - 1
    ext_sc[0:halo, :] = jnp.where(prev_ok, prev_ref[0], 0.0)
    ext_sc[halo:halo + tm, :] = x_ref[0]
    ext_sc[halo + tm:2 * halo + tm, :] = jnp.where(next_ok, next_ref[0], 0.0)
    acc = jnp.broadcast_to(b_ref[...], (tm, XBC_DIM))
    for k in range(CONV_W):
        off = halo - CONV_W // 2 + k
        acc = acc + w_ref[k:k + 1, :] * ext_sc[off:off + tm, :]
    o_ref[0] = _silu(acc)


def _conv(xbc, conv_w, conv_b, n_lat_tiles):
    b, lc, c = xbc.shape
    tm = TOKEN_TILE
    nt = lc // tm
    hb = tm // SUBLANES
    last = lc // SUBLANES - 1
    kern = functools.partial(_conv_kernel, tm=tm, n_lat_tiles=n_lat_tiles)
    return pl.pallas_call(
        kern,
        grid=(b, nt),
        in_specs=[pl.BlockSpec((1, tm, c), lambda bi, i: (bi, i, 0)),
                  pl.BlockSpec((1, SUBLANES, c), lambda bi, i: (bi, jnp.maximum(i * hb - 1, 0), 0)),
                  pl.BlockSpec((1, SUBLANES, c), lambda bi, i: (bi, jnp.minimum((i + 1) * hb, last), 0)),
                  pl.BlockSpec((SUBLANES, c), lambda bi, i: (0, 0)),
                  pl.BlockSpec((1, c), lambda bi, i: (0, 0))],
        out_specs=pl.BlockSpec((1, tm, c), lambda bi, i: (bi, i, 0)),
        out_shape=jax.ShapeDtypeStruct((b, lc, c), F32),
        scratch_shapes=[pltpu.VMEM((tm + 2 * SUBLANES, c), F32)],
        compiler_params=_params(("parallel", "parallel")),
        name="conv",
    )(xbc, xbc, xbc, conv_w, conv_b)


def _ssd_kernel(xbc_ref, misc_ref, bias_ref, alog_ref, alogt_ref, dskip_ref, y_ref, h_sc, *, direction):
    i = pl.program_id(1)
    q = CHUNK
    n = SSD_STATE
    pdim = SSD_HEAD_DIM

    @pl.when(i == 0)
    def _():
        h_sc[...] = jnp.zeros(h_sc.shape, F32)

    xbc = xbc_ref[0]
    xs = xbc[:, :SSD_INNER]
    bmat = xbc[:, SSD_INNER:SSD_INNER + SSD_GROUPS * n]
    cmat = xbc[:, SSD_INNER + SSD_GROUPS * n:]
    raw = misc_ref[0] + bias_ref[...]
    softplus = lambda v: jnp.maximum(v, 0.0) + jnp.log(1.0 + jnp.exp(-jnp.abs(v)))
    nd = 2 * SSD_HEADS
    lane = lax.broadcasted_iota(jnp.int32, (1, LANES), 1)
    a = softplus(raw) * jnp.where(lane < nd, -jnp.exp(alog_ref[...]), 0.0)
    dt_t = softplus(raw.T[0:nd, :])
    a_t = dt_t * (-jnp.exp(alogt_ref[...]))

    r_i = lax.broadcasted_iota(jnp.int32, (q, q), 0)
    c_i = lax.broadcasted_iota(jnp.int32, (q, q), 1)
    valid = (c_i <= r_i) if direction == 0 else (c_i >= r_i)
    tri = jnp.where(valid, 1.0, 0.0).astype(F32)
    tri_t = jnp.where(valid, 0.0, 1.0).astype(F32) + jnp.where(r_i == c_i, 1.0, 0.0).astype(F32)
    acs = jnp.dot(tri, a, precision=HIGHEST, preferred_element_type=F32)
    acs_t = jnp.dot(a_t, tri_t, precision=HIGHEST, preferred_element_type=F32)
    bt = bmat.T
    last = q - 1 if direction == 0 else 0
    xb = xs.astype(BF16)

    cb = [jnp.dot(cmat[:, g * n:(g + 1) * n].astype(BF16), bt[g * n:(g + 1) * n, :].astype(BF16),
                  preferred_element_type=F32) for g in range(SSD_GROUPS)]
    ys = []
    for hd in range(SSD_HEADS):
        j = direction * SSD_HEADS + hd
        g = hd // (SSD_HEADS // SSD_GROUPS)
        col = jnp.broadcast_to(acs[:, j:j + 1], (q, q))
        row = acs_t[j:j + 1, :]
        dt_row = dt_t[j:j + 1, :]
        atot = acs_t[j:j + 1, last:last + 1]
        decay = jnp.exp(jnp.where(valid, col - row, NEG_BIG))
        xh = xb[:, hd * pdim:(hd + 1) * pdim]
        h_prev = h_sc[hd]
        c_g = cmat[:, g * n:(g + 1) * n]
        y = (jnp.dot((cb[g] * decay * dt_row).astype(BF16), xh, preferred_element_type=F32)
             + jnp.dot((c_g * jnp.exp(col[:, :n])).astype(BF16), h_prev.astype(BF16),
                       preferred_element_type=F32))
        if direction == 0:
            y = y + dskip_ref[:, hd * pdim:(hd + 1) * pdim] * xs[:, hd * pdim:(hd + 1) * pdim]
        ys.append(y)
        bw = (bt[g * n:(g + 1) * n, :] * (jnp.exp(atot - row) * dt_row)).astype(BF16)
        h_sc[hd] = jnp.exp(atot) * h_prev + jnp.dot(bw, xh, preferred_element_type=F32)
    y_ref[0] = jnp.concatenate(ys, axis=-1)


def _ssd(xbc_act, misc, bias_pad, alog_pad, dskip_wide, direction, n_lat_chunks):
    b, lc, c = xbc_act.shape
    alog_t = jnp.broadcast_to(alog_pad[0, :2 * SSD_HEADS, None], (2 * SSD_HEADS, LANES))
    nc = lc // CHUNK
    if direction == 0:
        cmap = lambda bi, i: (bi, (i + n_lat_chunks) % nc, 0)
    else:
        cmap = lambda bi, i: (bi, nc - 1 - i, 0)
    const2 = lambda bi, i: (0, 0)
    kern = functools.partial(_ssd_kernel, direction=direction)
    return pl.pallas_call(
        kern,
        grid=(b, nc),
        in_specs=[pl.BlockSpec((1, CHUNK, c), cmap),
                  pl.BlockSpec((1, CHUNK, LANES), cmap),
                  pl.BlockSpec((1, LANES), const2),
                  pl.BlockSpec((1, LANES), const2),
                  pl.BlockSpec((2 * SSD_HEADS, LANES), const2),
                  pl.BlockSpec((1, SSD_INNER), const2)],
        out_specs=pl.BlockSpec((1, CHUNK, SSD_INNER), cmap),
        out_shape=jax.ShapeDtypeStruct((b, lc, SSD_INNER), F32),
        scratch_shapes=[pltpu.VMEM((SSD_HEADS, SSD_STATE, SSD_HEAD_DIM), F32)],
        compiler_params=_params(("parallel", "arbitrary")),
        name=f"ssd_dir{direction}",
    )(xbc_act, misc, bias_pad, alog_pad, alog_t, dskip_wide)


def _mixout_kernel(att_ref, y0_ref, y1_ref, z_ref, x_ref, mod_ref, sg_ref, wout_ref, g2_ref,
                   rwh_ref, rwl_ref, rb_ref, xn_ref, h2_ref, gate_ref, cnt_ref):
    y = (y0_ref[0] + y1_ref[0]) * _silu(z_ref[0])
    ssd = _rms(y, sg_ref[...])
    half = MLA_HEADS * V_DIM
    o = (jnp.dot(att_ref[0], wout_ref[0:half, :], preferred_element_type=F32)
         + jnp.dot(ssd.astype(BF16), wout_ref[half:, :], preferred_element_type=F32))
    xn = x_ref[0] + mod_ref[0, 0, 2:3, :] * o
    xn_ref[0] = xn
    h2 = _rms(xn, g2_ref[...]) * (1.0 + mod_ref[0, 0, 4:5, :]) + mod_ref[0, 0, 3:4, :]
    h2_ref[0] = h2.astype(BF16)

    h2_hi = h2.astype(BF16)
    h2_lo = (h2 - h2_hi.astype(F32)).astype(BF16)
    logits = (jnp.dot(h2_hi, rwh_ref[...], preferred_element_type=F32)
              + jnp.dot(h2_lo, rwh_ref[...], preferred_element_type=F32)
              + jnp.dot(h2_hi, rwl_ref[...], preferred_element_type=F32)) + rb_ref[...]
    lane = lax.broadcasted_iota(jnp.int32, logits.shape, 1)
    is_grp = jnp.logical_and(lane >= N_EXPERTS, lane < N_EXPERTS + N_GROUPS)
    l1 = jnp.where(is_grp, logits, NEG_BIG)
    m1 = jnp.max(l1, axis=-1, keepdims=True)
    p_top = 1.0 / jnp.sum(jnp.exp(l1 - m1), axis=-1, keepdims=True)
    g_idx = jnp.min(jnp.where(jnp.logical_and(is_grp, l1 == m1), lane - N_EXPERTS, N_GROUPS),
                    axis=-1, keepdims=True)
    sel = jnp.logical_and(lane < N_EXPERTS, (lane // EXPERTS_PER_GROUP) == g_idx)
    l2 = jnp.where(sel, logits, NEG_BIG)
    m2 = jnp.max(l2, axis=-1, keepdims=True)
    i_first = jnp.min(jnp.where(jnp.logical_and(sel, l2 == m2), lane, LANES), axis=-1, keepdims=True)
    l3 = jnp.where(lane == i_first, NEG_BIG, l2)
    m3 = jnp.max(l3, axis=-1, keepdims=True)
    i_second = jnp.min(jnp.where(jnp.logical_and(sel, l3 == m3), lane, LANES), axis=-1, keepdims=True)
    e2 = jnp.exp(m3 - m2)
    inv = p_top / (1.0 + e2)
    gate = jnp.where(lane == i_first, inv, jnp.where(lane == i_second, e2 * inv, 0.0))
    grp_hot = jnp.where(jnp.logical_and(is_grp, lane - N_EXPERTS == g_idx), 1.0, 0.0)
    gate_ref[0] = gate + grp_hot
    cnt_ref[0, 0] = jnp.broadcast_to(jnp.sum(grp_hot, axis=0, keepdims=True), (SUBLANES, LANES))


def _mixout(att, y0, y1, z, xc, modsel, sg, wout, g2, rw, rb, n_lat_tiles):
    rw_hi = rw.astype(BF16)
    rw_lo = (rw - rw_hi.astype(F32)).astype(BF16)
    b, lc, d = xc.shape
    tm = TOKEN_TILE
    nt = lc // tm
    tok = lambda w: pl.BlockSpec((1, tm, w), lambda bi, i: (bi, i, 0))
    const2 = lambda bi, i: (0, 0)
    return pl.pallas_call(
        _mixout_kernel,
        grid=(b, nt),
        in_specs=[tok(SSD_INNER), tok(SSD_INNER), tok(SSD_INNER), tok(SSD_INNER), tok(d),
                  pl.BlockSpec((1, 1, SUBLANES, d), lambda bi, i: (bi, i // n_lat_tiles, 0, 0)),
                  pl.BlockSpec((1, SSD_INNER), const2),
                  pl.BlockSpec(wout.shape, const2),
                  pl.BlockSpec((1, d), const2),
                  pl.BlockSpec((d, LANES), const2),
                  pl.BlockSpec((d, LANES), const2),
                  pl.BlockSpec((1, LANES), const2)],
        out_specs=[tok(d), tok(d), tok(LANES),
                   pl.BlockSpec((1, 1, SUBLANES, LANES), lambda bi, i: (bi, i, 0, 0))],
        out_shape=[jax.ShapeDtypeStruct((b, lc, d), F32),
                   jax.ShapeDtypeStruct((b, lc, d), BF16),
                   jax.ShapeDtypeStruct((b, lc, LANES), F32),
                   jax.ShapeDtypeStruct((b, nt, SUBLANES, LANES), F32)],
        compiler_params=_params(("parallel", "parallel")),
        name="mixout",
    )(att, y0, y1, z, xc, modsel, sg, wout, g2, rw_hi, rw_lo, rb)


def _moe_kernel(off_ref, h2_ref, gate_ref, wgu_ref, wd_ref, y_ref, hs_sc, gs_sc, ys_sc, pt_sc,
                *, tmm, nt, win):
    bi = pl.program_id(0)
    t = pl.program_id(1)
    s = pl.program_id(2)
    base = (bi * nt + t) * SUBLANES
    nblk = tmm // MOE_BLOCK

    @pl.when(s == 0)
    def _():
        gate = gate_ref[0]
        lane = lax.broadcasted_iota(jnp.int32, (tmm, LANES), 1)
        hot = jnp.where(jnp.logical_and(lane >= GRP_LANE, lane < GRP_LANE + N_GROUPS), gate, 0.0)
        r_i = lax.broadcasted_iota(jnp.int32, (MOE_BLOCK, MOE_BLOCK), 0)
        c_i = lax.broadcasted_iota(jnp.int32, (MOE_BLOCK, MOE_BLOCK), 1)
        tri = jnp.where(c_i < r_i, 1.0, 0.0).astype(BF16)
        run = jnp.zeros((1, LANES), F32)
        cums = []
        for blk in range(nblk):
            hb = hot[blk * MOE_BLOCK:(blk + 1) * MOE_BLOCK]
            cums.append(jnp.dot(tri, hb.astype(BF16), preferred_element_type=F32) + run)
            run = run + jnp.sum(hb, axis=0, keepdims=True)
        cum = jnp.concatenate(cums, axis=0)
        lane1 = lax.broadcasted_iota(jnp.int32, (1, LANES), 1)
        offv = jnp.zeros((1, LANES), F32)
        for g in range(N_GROUPS):
            offv = jnp.where(lane1 == GRP_LANE + g, off_ref[base + g].astype(F32), offv)
        pos = jnp.sum(hot * (cum + offv), axis=-1, keepdims=True)
        dst = lax.broadcasted_iota(jnp.int32, (tmm, tmm), 1).astype(F32)
        pt_sc[...] = jnp.where(pos == dst, 1.0, 0.0).astype(BF16)
        pos_row = jnp.broadcast_to(pos, (tmm, LANES)).T[0:1, :]
        src = lax.broadcasted_iota(jnp.int32, (tmm, tmm), 0).astype(F32)
        perm = jnp.where(pos_row == src, 1.0, 0.0).astype(BF16)
        hs_sc[...] = jnp.dot(perm, h2_ref[0], preferred_element_type=F32).astype(BF16)
        g_hi = gate.astype(BF16)
        g_lo = (gate - g_hi.astype(F32)).astype(BF16)
        g2 = jnp.dot(perm, jnp.concatenate([g_hi, g_lo], axis=-1), preferred_element_type=F32)
        gs_sc[...] = g2[:, :LANES] + g2[:, LANES:]
        ys_sc[...] = jnp.zeros(ys_sc.shape, F32)

    grp = s // (EXPERTS_PER_GROUP // MOE_STEP_EXPERTS)
    lo = off_ref[base + grp]
    hi = off_ref[base + grp + 1]
    start = (lo // BF16_ROWS) * BF16_ROWS
    lane_b = lax.broadcasted_iota(jnp.int32, (win, LANES), 1)
    row_b = lax.broadcasted_iota(jnp.int32, (win, 1), 0)

    def window(k, carry):
        first = start + k * win
        r0 = pl.multiple_of(jnp.minimum(first, tmm - win), BF16_ROWS)
        hb = hs_sc[pl.ds(r0, win), :]
        gb = gs_sc[pl.ds(r0, win), :]
        live = r0 + row_b >= first
        acc = jnp.zeros((win, ys_sc.shape[1]), F32)
        for e in range(MOE_STEP_EXPERTS):
            gcol = jnp.sum(jnp.where(lane_b == s * MOE_STEP_EXPERTS + e, gb, 0.0), axis=-1, keepdims=True)
            gcol = jnp.where(live, gcol, 0.0)
            hgu = jnp.dot(hb, wgu_ref[e], preferred_element_type=F32)
            act = (_silu(hgu[:, :EXPERT_FF]) * hgu[:, EXPERT_FF:] * gcol).astype(BF16)
            acc = acc + jnp.dot(act, wd_ref[e], preferred_element_type=F32)
        ys_sc[pl.ds(r0, win), :] += acc
        return carry

    lax.fori_loop(0, (hi - start + win - 1) // win, window, 0)

    @pl.when(s == N_EXPERTS // MOE_STEP_EXPERTS - 1)
    def _():
        y_ref[0] = jnp.dot(pt_sc[...], ys_sc[...].astype(BF16), preferred_element_type=F32).astype(BF16)


def _moe(h2, gate, cnt, wgu, wd, rows, tmm):
    b, _, d = h2.shape
    nt = rows // tmm
    per = tmm // TOKEN_TILE
    n = cnt[:, :nt * per, 0, GRP_LANE:GRP_LANE + N_GROUPS].reshape(b, nt, per, N_GROUPS).sum(axis=2)
    ends = jnp.cumsum(n.astype(jnp.int32), axis=-1)
    zeros = jnp.zeros((b, nt, 1), jnp.int32)
    off = jnp.concatenate([zeros, ends] + [zeros] * (SUBLANES - 1 - N_GROUPS), axis=-1).reshape(-1)
    win = min(tmm, -(-(tmm * 3 // 10) // 64) * 64)
    kern = functools.partial(_moe_kernel, tmm=tmm, nt=nt, win=win)
    tok = lambda w: pl.BlockSpec((1, tmm, w), lambda bi, t, s, o: (bi, t, 0))
    wspec = lambda r, c: pl.BlockSpec((MOE_STEP_EXPERTS, r, c), lambda bi, t, s, o: (s, 0, 0))
    return pl.pallas_call(
        kern,
        grid_spec=pltpu.PrefetchScalarGridSpec(
            num_scalar_prefetch=1,
            grid=(b, nt, N_EXPERTS // MOE_STEP_EXPERTS),
            in_specs=[tok(d), tok(LANES), wspec(d, 2 * EXPERT_FF), wspec(EXPERT_FF, d)],
            out_specs=tok(d),
            scratch_shapes=[pltpu.VMEM((tmm, d), BF16), pltpu.VMEM((tmm, LANES), F32),
                            pltpu.VMEM((tmm, d), F32), pltpu.VMEM((tmm, tmm), BF16)]),
        out_shape=jax.ShapeDtypeStruct((b, rows, d), BF16),
        compiler_params=_params(("parallel", "parallel", "arbitrary"), vmem_mb=56),
        name=f"moe_t{tmm}",
    )(off, h2, gate, wgu, wd)


def _resid_kernel(xn_ref, y_ref, mod_ref, fg_ref, o_ref, *, final):
    out = xn_ref[0] + mod_ref[0, 0, 5:6, :] * y_ref[0].astype(F32)
    if final:
        out = _rms(out, fg_ref[...])
    o_ref[0] = out


def _resid(xn, y, modsel, fg, rows, n_lat_tiles, final):
    b, _, d = xn.shape
    tm = TOKEN_TILE
    tok = pl.BlockSpec((1, tm, d), lambda bi, i: (bi, i, 0))
    return pl.pallas_call(
        functools.partial(_resid_kernel, final=final),
        grid=(b, rows // tm),
        in_specs=[tok, tok,
                  pl.BlockSpec((1, 1, SUBLANES, d), lambda bi, i: (bi, i // n_lat_tiles, 0, 0)),
                  pl.BlockSpec((1, d), lambda bi, i: (0, 0))],
        out_specs=tok,
        out_shape=jax.ShapeDtypeStruct((b, rows, d), F32),
        compiler_params=_params(("parallel", "parallel")),
        name="resid_final" if final else "resid",
    )(xn, y, modsel, fg)


def _rope_tables(seq, ctx_len):
    rows = seq // GRID_W
    row = jnp.repeat(jnp.arange(rows), GRID_W)
    col = jnp.tile(jnp.arange(GRID_W), rows)
    inv_freq = ROPE_THETA ** (-jnp.arange(ROPE_DIM // 4, dtype=F32) / (ROPE_DIM // 4))
    ang = jnp.concatenate([row[:, None] * inv_freq, col[:, None] * inv_freq], axis=-1)
    cos, sin = jnp.cos(ang), jnp.sin(ang)
    c32 = jnp.concatenate([cos, cos], axis=-1)
    s32 = jnp.concatenate([-sin, sin], axis=-1)
    c32 = jnp.concatenate([c32, jnp.ones((ctx_len, ROPE_DIM), F32)], axis=0)
    s32 = jnp.concatenate([s32, jnp.zeros((ctx_len, ROPE_DIM), F32)], axis=0)
    pad = ((0, 0), (0, LANES - ROPE_DIM))
    return jnp.concatenate([jnp.pad(c32, pad), jnp.pad(s32, pad)], axis=-1)


def _swap_halves(w):
    lead = w.shape[:-1]
    w4 = w.reshape(lead + (-1, 2, ROPE_DIM // 2))
    return w4[..., ::-1, :].reshape(w.shape)


def kernel(x, c, ctx, c_ctx, w_mod, b_mod, norm1_g, norm2_g, w_in, q_norm_g, w_qb, kv_norm_g, w_kvb,
           conv_w, conv_b, a_log, dt_bias, d_skip, ssd_norm_g, w_out, router_w1, router_b1,
           router_w2, router_b2, w_gate, w_up, w_down, final_g):
    b, seq, d = x.shape
    ctx_len = ctx.shape[1]
    depth = w_mod.shape[0]
    assert ctx_len == TOKEN_TILE and seq % TOKEN_TILE == 0 and seq % GRID_W == 0
    n_lat_tiles = seq // TOKEN_TILE
    n_lat_chunks = seq // CHUNK

    xc = jnp.concatenate([x, ctx], axis=1)
    tab = _rope_tables(seq, ctx_len)

    cvec = jnp.zeros((SUBLANES, d), F32).at[:b].set(c).at[b].set(c_ctx)
    mod = _modulation(cvec, w_mod, b_mod).reshape(depth, SUBLANES, 6, d)

    for l in range(depth):
        last_layer = l == depth - 1
        ml = mod[l]
        modsel = jnp.stack([ml[:b], jnp.broadcast_to(ml[b], (b, 6, d))], axis=1)
        modsel = jnp.pad(modsel, ((0, 0), (0, 0), (0, SUBLANES - 6), (0, 0)))

        w = w_in[l]
        o_kva, o_kr = Q_LORA, Q_LORA + KV_LORA
        o_z = o_kr + ROPE_DIM
        o_xbc, o_dt = o_z + SSD_INNER, o_z + SSD_INNER + XBC_DIM
        w_kr = w[:, o_kr:o_z]
        lane_pad = lambda t: jnp.pad(t, [(0, 0)] * (t.ndim - 1) + [(0, LANES - t.shape[-1])])
        win = jnp.concatenate([w[:, :o_kr], w[:, o_z:o_dt], lane_pad(w[:, o_dt:]), lane_pad(w_kr),
                               lane_pad(_swap_halves(w_kr))], axis=-1).astype(BF16)
        wq3 = w_qb[l].reshape(Q_LORA, MLA_HEADS, NOPE_DIM + ROPE_DIM)
        wq_rope = wq3[:, :, NOPE_DIM:]
        wq = jnp.concatenate([wq3[:, :, :NOPE_DIM].reshape(Q_LORA, MLA_HEADS * NOPE_DIM),
                              lane_pad(wq_rope).reshape(Q_LORA, MLA_HEADS * LANES),
                              lane_pad(_swap_halves(wq_rope)).reshape(Q_LORA, MLA_HEADS * LANES)],
                             axis=-1).astype(BF16)
        wkv3 = w_kvb[l].reshape(KV_LORA, MLA_HEADS, NOPE_DIM + V_DIM)
        wkbt = jnp.transpose(wkv3[:, :, :NOPE_DIM], (1, 2, 0)).astype(BF16)
        wvb = jnp.transpose(wkv3[:, :, NOPE_DIM:], (1, 0, 2)).astype(BF16)

        q, kv, z, xbc, misc = _inproj(xc, modsel, norm1_g[l][None], win, q_norm_g[l][None],
                                      kv_norm_g[l][None], wq, wkbt, tab, n_lat_tiles)
        att = _attention(q, kv, wvb, n_lat_tiles, ctx_len)

        cw = jnp.pad(conv_w[l], ((0, SUBLANES - CONV_W), (0, 0)))
        xbc_act = _conv(xbc, cw, conv_b[l][None], n_lat_tiles)
        bias_pad = jnp.pad(dt_bias[l].reshape(1, -1), ((0, 0), (0, LANES - 2 * SSD_HEADS)))
        alog_pad = jnp.pad(a_log[l].reshape(1, -1), ((0, 0), (0, LANES - 2 * SSD_HEADS)))
        dskip_wide = jnp.repeat(d_skip[l], SSD_HEAD_DIM)[None]
        y0 = _ssd(xbc_act, misc, bias_pad, alog_pad, dskip_wide, 0, n_lat_chunks)
        y1 = _ssd(xbc_act, misc, bias_pad, alog_pad, dskip_wide, 1, n_lat_chunks)

        rw = jnp.pad(jnp.concatenate([router_w2[l], router_w1[l]], axis=-1),
                     ((0, 0), (0, LANES - N_EXPERTS - N_GROUPS)))
        rb = jnp.pad(jnp.concatenate([router_b2[l], router_b1[l]])[None],
                     ((0, 0), (0, LANES - N_EXPERTS - N_GROUPS)))
        xn, h2, gate, cnt = _mixout(att, y0, y1, z, xc, modsel, ssd_norm_g[l][None], w_out[l].astype(BF16),
                                    norm2_g[l][None], rw, rb, n_lat_tiles)
        rows = seq if last_layer else seq + ctx_len
        tmm = _pick(rows, (1280, 1024, 768, 512, 256))
        wgu = jnp.concatenate([w_gate[l].astype(BF16), w_up[l].astype(BF16)], axis=-1)
        y = _moe(h2, gate, cnt, wgu, w_down[l].astype(BF16), rows, tmm)
        xc = _resid(xn, y, modsel, final_g[None], rows, n_lat_tiles, last_layer)
    return xc
```

```python
import functools

import jax
import jax.numpy as jnp
from jax import lax
from jax.experimental import pallas as pl
from jax.experimental.pallas import tpu as pltpu

F32 = jnp.float32
BF16 = jnp.bfloat16
HIGHEST = lax.Precision.HIGHEST

GRID_W = 64
MLA_HEADS = 8
Q_LORA = 256
KV_LORA = 128
NOPE_DIM = 64
ROPE_DIM = 32
V_DIM = 64
ROPE_THETA = 10000.0
SSD_HEADS = 8
SSD_HEAD_DIM = 64
SSD_INNER = SSD_HEADS * SSD_HEAD_DIM
SSD_GROUPS = 2
SSD_STATE = 64
CONV_W = 5
CHUNK = 128
XBC_DIM = SSD_INNER + 2 * SSD_GROUPS * SSD_STATE
N_GROUPS = 4
EXPERTS_PER_GROUP = 8
N_EXPERTS = N_GROUPS * EXPERTS_PER_GROUP
EXPERT_FF = 256
EPS = 1e-6

LANES = 128
SUBLANES = 8
TOKEN_TILE = 256
KQ_DIM = KV_LORA + ROPE_DIM
ATT_SCALE = (NOPE_DIM + ROPE_DIM) ** -0.5
NEG_BIG = -1e30

COL_QA = 0
COL_KVA = COL_QA + Q_LORA
COL_Z = COL_KVA + KV_LORA
COL_XBC = COL_Z + SSD_INNER
COL_MISC = COL_XBC + XBC_DIM
COL_KR = COL_MISC + LANES
COL_KRP = COL_KR + LANES
IN_COLS_PAD = COL_KRP + LANES
KQ_PAD = 2 * LANES
ONE_LANE = ROPE_DIM
LOG2E = 1.4426950408889634
ATT_CHUNK = 512
ATT_ROW_SPLIT = 4
SSD_STEP_CHUNKS = 2
MIX_SLABS = 1
MOE_BLOCK = 128
BF16_ROWS = 16
MOE_STEP_EXPERTS = 4
GRP_LANE = N_EXPERTS


def _params(sem, vmem_mb=48):
    return pltpu.CompilerParams(dimension_semantics=sem, vmem_limit_bytes=vmem_mb * 1024 * 1024)


def _pick(n, cands):
    for c in cands:
        if n % c == 0:
            return c
    raise ValueError(f"no tile for {n} in {cands}")


def _rms(x, g):
    return x * lax.rsqrt(jnp.mean(x * x, axis=-1, keepdims=True) + EPS) * g


def _silu(x):
    return x * (1.0 / (1.0 + jnp.exp(-x)))


def _lane_tile(x, n):
    return jnp.concatenate([x] * n, axis=1)


def _bdot(a, b):
    return jnp.dot(a.astype(BF16), b.astype(BF16), preferred_element_type=F32)


def _mod_kernel(c_ref, w_ref, b_ref, o_ref):
    s = _silu(c_ref[...])
    o_ref[0] = jnp.dot(s, w_ref[0], precision=HIGHEST, preferred_element_type=F32) + b_ref[0]


def _modulation(cvec, w_mod, b_mod):
    depth, d, n = w_mod.shape
    tn = _pick(n, (768, 512, 256, 128))
    return pl.pallas_call(
        _mod_kernel,
        grid=(depth, n // tn),
        in_specs=[pl.BlockSpec((SUBLANES, d), lambda l, j: (0, 0)),
                  pl.BlockSpec((1, d, tn), lambda l, j: (l, 0, j)),
                  pl.BlockSpec((1, 1, tn), lambda l, j: (l, 0, j))],
        out_specs=pl.BlockSpec((1, SUBLANES, tn), lambda l, j: (l, 0, j)),
        out_shape=jax.ShapeDtypeStruct((depth, SUBLANES, n), F32),
        compiler_params=_params(("parallel", "parallel")),
        name="modulation",
    )(cvec, w_mod, b_mod.reshape(depth, 1, n))


def _inproj_kernel(x_ref, xprev_ref, xnext_ref, mod_ref, g_ref, win_ref, qg_ref, kvg_ref, wq_ref, wkbt_ref,
                   tab_ref, cw_ref, cb_ref, q_ref, kv_ref, z_ref, xbc_ref, misc_ref, ext_sc,
                   *, tm, n_lat_tiles):
    i = pl.program_id(1)
    shift = mod_ref[0, 0, 0:1, :]
    scale = mod_ref[0, 0, 1:2, :]
    modnorm = lambda v: _rms(v, g_ref[...]) * (1.0 + scale) + shift
    p = _bdot(modnorm(x_ref[0]), win_ref[...])
    z_ref[0] = p[:, COL_Z:COL_XBC]
    misc_ref[0] = p[:, COL_MISC:COL_KR]

    halo = SUBLANES
    h_halo = modnorm(jnp.concatenate([xprev_ref[0], xnext_ref[0]], axis=0))
    p_halo = _bdot(h_halo, win_ref[:, COL_XBC:COL_MISC])
    prev_ok = jnp.logical_and(i >= 1, i < n_lat_tiles)
    next_ok = i < n_lat_tiles - 1
    ext_sc[0:halo, :] = jnp.where(prev_ok, p_halo[0:halo], 0.0)
    ext_sc[halo:halo + tm, :] = p[:, COL_XBC:COL_MISC]
    ext_sc[halo + tm:2 * halo + tm, :] = jnp.where(next_ok, p_halo[halo:], 0.0)
    conv = jnp.broadcast_to(cb_ref[...], (tm, XBC_DIM))
    for k in range(CONV_W):
        off = halo - CONV_W // 2 + k
        conv = conv + cw_ref[k:k + 1, :] * ext_sc[off:off + tm, :]
    xbc_ref[0] = _silu(conv)

    rope_c = tab_ref[:, :LANES]
    rope_s = tab_ref[:, LANES:]
    lane = lax.broadcasted_iota(jnp.int32, (1, LANES), 1)
    one_col = jnp.where(lane == ONE_LANE, 1.0, 0.0)

    ckv = _rms(p[:, COL_KVA:COL_Z], kvg_ref[...])
    k_rope = p[:, COL_KR:COL_KRP] * rope_c + p[:, COL_KRP:IN_COLS_PAD] * rope_s + one_col
    kv_ref[0] = jnp.concatenate([ckv, k_rope], axis=-1).astype(BF16)

    qn = _rms(p[:, COL_QA:COL_KVA], qg_ref[...])
    q3 = _bdot(qn, wq_ref[...])
    n_nope = MLA_HEADS * NOPE_DIM
    n_grp = MLA_HEADS * LANES
    q_scale = ATT_SCALE * LOG2E
    q_nope = (q3[:, :n_nope] * q_scale).astype(BF16)
    for hd in range(MLA_HEADS):
        q_abs = jnp.dot(q_nope[:, hd * NOPE_DIM:(hd + 1) * NOPE_DIM], wkbt_ref[hd],
                        preferred_element_type=F32)
        lo = n_nope + hd * LANES
        q_rope = (q3[:, lo:lo + LANES] * rope_c + q3[:, lo + n_grp:lo + n_grp + LANES] * rope_s) * q_scale
        q_ref[0, hd] = jnp.concatenate([q_abs, q_rope], axis=-1).astype(BF16)


def _inproj(xc, modsel, g1, win, qg, kvg, wq, wkbt, tab, cw, cb, n_lat_tiles):
    b, lc, d = xc.shape
    tm = TOKEN_TILE
    nt = lc // tm
    hb = tm // SUBLANES
    last = lc // SUBLANES - 1
    const2 = lambda bi, i: (0, 0)
    return pl.pallas_call(
        functools.partial(_inproj_kernel, tm=tm, n_lat_tiles=n_lat_tiles),
        grid=(b, nt),
        in_specs=[pl.BlockSpec((1, tm, d), lambda bi, i: (bi, i, 0)),
                  pl.BlockSpec((1, SUBLANES, d), lambda bi, i: (bi, jnp.maximum(i * hb - 1, 0), 0)),
                  pl.BlockSpec((1, SUBLANES, d), lambda bi, i: (bi, jnp.minimum((i + 1) * hb, last), 0)),
                  pl.BlockSpec((1, 1, SUBLANES, d), lambda bi, i: (bi, i // n_lat_tiles, 0, 0)),
                  pl.BlockSpec((1, d), const2),
                  pl.BlockSpec((d, IN_COLS_PAD), const2),
                  pl.BlockSpec((1, Q_LORA), const2),
                  pl.BlockSpec((1, KV_LORA), const2),
                  pl.BlockSpec(wq.shape, const2),
                  pl.BlockSpec(wkbt.shape, lambda bi, i: (0, 0, 0)),
                  pl.BlockSpec((tm, tab.shape[1]), lambda bi, i: (i, 0)),
                  pl.BlockSpec((SUBLANES, XBC_DIM), const2),
                  pl.BlockSpec((1, XBC_DIM), const2)],
        out_specs=[pl.BlockSpec((1, MLA_HEADS, tm, KQ_PAD), lambda bi, i: (bi, 0, i, 0)),
                   pl.BlockSpec((1, tm, KQ_PAD), lambda bi, i: (bi, i, 0)),
                   pl.BlockSpec((1, tm, SSD_INNER), lambda bi, i: (bi, i, 0)),
                   pl.BlockSpec((1, tm, XBC_DIM), lambda bi, i: (bi, i, 0)),
                   pl.BlockSpec((1, tm, LANES), lambda bi, i: (bi, i, 0))],
        out_shape=[jax.ShapeDtypeStruct((b, MLA_HEADS, lc, KQ_PAD), BF16),
                   jax.ShapeDtypeStruct((b, lc, KQ_PAD), BF16),
                   jax.ShapeDtypeStruct((b, lc, SSD_INNER), F32),
                   jax.ShapeDtypeStruct((b, lc, XBC_DIM), F32),
                   jax.ShapeDtypeStruct((b, lc, LANES), F32)],
        scratch_shapes=[pltpu.VMEM((tm + 2 * SUBLANES, XBC_DIM), F32)],
        compiler_params=_params(("parallel", "parallel")),
        name="inproj",
    )(xc, xc, xc, modsel, g1, win, qg, kvg, wq, wkbt, tab, cw, cb)


def _attn_kernel(q_ref, kv_ref, wvb_ref, o_ref, m_sc, acc_sc, *, tq, tk, tc, row_split, nk, n_lat_q, ctx_len):
    qi = pl.program_id(1)
    kj = pl.program_id(2)
    rows = MLA_HEADS * tq

    @pl.when(kj == 0)
    def _():
        m_sc[...] = jnp.full(m_sc.shape, NEG_BIG, F32)
        acc_sc[...] = jnp.zeros(acc_sc.shape, F32)

    def chunk(q, r0, lo, width):
        nr = q.shape[0]
        kvc = kv_ref[0, lo:lo + width, :]
        s = lax.dot_general(q, kvc, (((1,), (1,)), ((), ())), preferred_element_type=F32)
        m_prev = m_sc[r0:r0 + nr, :]
        m_new = jnp.maximum(m_prev, jnp.max(s, axis=-1, keepdims=True))
        alpha = jnp.exp2(m_prev - m_new)
        p = jnp.exp2(s - _lane_tile(m_new, width // LANES))
        acc_sc[r0:r0 + nr, :] = (_lane_tile(alpha, KQ_PAD // LANES) * acc_sc[r0:r0 + nr, :]
                                 + jnp.dot(p.astype(BF16), kvc, preferred_element_type=F32))
        m_sc[r0:r0 + nr, :] = m_new

    def step(ctx_only):
        nr = rows // row_split
        qs = [q_ref[0, g * (MLA_HEADS // row_split):(g + 1) * (MLA_HEADS // row_split)].reshape(nr, KQ_PAD)
              for g in range(row_split)]
        spans = ([(tk - ctx_len, ctx_len)] if ctx_only
                 else [(lo, min(tc, tk - lo)) for lo in range(0, tk, tc)])
        for lo, width in spans:
            for g in range(row_split):
                chunk(qs[g], g * nr, lo, width)

    is_ctx_q = qi >= n_lat_q
    pl.when(jnp.logical_not(is_ctx_q))(functools.partial(step, False))
    pl.when(jnp.logical_and(is_ctx_q, kj == nk - 1))(functools.partial(step, True))

    @pl.when(kj == nk - 1)
    def _():
        acc = acc_sc[...]
        denom = acc[:, LANES + ONE_LANE:LANES + ONE_LANE + 1]
        o = (acc[:, :KV_LORA] / denom).astype(BF16)
        outs = [jnp.dot(o[hd * tq:(hd + 1) * tq], wvb_ref[hd], preferred_element_type=F32)
                for hd in range(MLA_HEADS)]
        o_ref[0] = jnp.concatenate(outs, axis=-1).astype(BF16)


def _attention(q, kv, wvb, n_lat_q, ctx_len):
    b, _, lc, _ = q.shape
    tq = TOKEN_TILE
    tk = _pick(lc, (3328, 1280, 768, 512, 256))
    tc = ATT_CHUNK
    assert ctx_len <= tk and ctx_len == TOKEN_TILE
    nq, nk = lc // tq, lc // tk
    rows = MLA_HEADS * tq
    kern = functools.partial(_attn_kernel, tq=tq, tk=tk, tc=tc, row_split=ATT_ROW_SPLIT, nk=nk, n_lat_q=n_lat_q, ctx_len=ctx_len)
    return pl.pallas_call(
        kern,
        grid=(b, nq, nk),
        in_specs=[pl.BlockSpec((1, MLA_HEADS, tq, KQ_PAD), lambda bi, i, j: (bi, 0, i, 0)),
                  pl.BlockSpec((1, tk, KQ_PAD), lambda bi, i, j: (bi, j, 0)),
                  pl.BlockSpec(wvb.shape, lambda bi, i, j: (0, 0, 0))],
        out_specs=pl.BlockSpec((1, tq, MLA_HEADS * V_DIM), lambda bi, i, j: (bi, i, 0)),
        out_shape=jax.ShapeDtypeStruct((b, lc, MLA_HEADS * V_DIM), BF16),
        scratch_shapes=[pltpu.VMEM((rows, LANES), F32), pltpu.VMEM((rows, KQ_PAD), F32)],
        compiler_params=_params(("parallel", "parallel", "arbitrary")),
        name="attention",
    )(q, kv, wvb)


def _ssd_kernel(xbc_ref, misc_ref, bias_ref, alog_ref, alogt_ref, dskip_ref, y_ref, h_sc, *, direction):
    i = pl.program_id(1)
    q = CHUNK
    n = SSD_STATE
    pdim = SSD_HEAD_DIM

    @pl.when(i == 0)
    def _():
        h_sc[...] = jnp.zeros(h_sc.shape, F32)

    softplus = lambda v: jnp.maximum(v, 0.0) + jnp.log(1.0 + jnp.exp(-jnp.abs(v)))
    nd = 2 * SSD_HEADS
    lane = lax.broadcasted_iota(jnp.int32, (1, LANES), 1)
    r_i = lax.broadcasted_iota(jnp.int32, (q, q), 0)
    c_i = lax.broadcasted_iota(jnp.int32, (q, q), 1)
    valid = (c_i <= r_i) if direction == 0 else (c_i >= r_i)
    tri = jnp.where(valid, 1.0, 0.0).astype(F32)
    tri_t = jnp.where(valid, 0.0, 1.0).astype(F32) + jnp.where(r_i == c_i, 1.0, 0.0).astype(F32)
    last = q - 1 if direction == 0 else 0

    def one_chunk(r0):
        xbc = xbc_ref[0, r0:r0 + q, :]
        xs = xbc[:, :SSD_INNER]
        bmat = xbc[:, SSD_INNER:SSD_INNER + SSD_GROUPS * n]
        cmat = xbc[:, SSD_INNER + SSD_GROUPS * n:]
        raw = misc_ref[0, r0:r0 + q, :] + bias_ref[...]
        a = softplus(raw) * jnp.where(lane < nd, -jnp.exp(alog_ref[...]), 0.0)
        dt_t = softplus(raw.T[0:nd, :])
        a_t = dt_t * (-jnp.exp(alogt_ref[...]))
        acs = jnp.dot(tri, a, precision=HIGHEST, preferred_element_type=F32)
        acs_t = jnp.dot(a_t, tri_t, precision=HIGHEST, preferred_element_type=F32)
        bt = bmat.T
        xb = xs.astype(BF16)

        cb = [jnp.dot(cmat[:, g * n:(g + 1) * n].astype(BF16), bt[g * n:(g + 1) * n, :].astype(BF16),
                      preferred_element_type=F32) for g in range(SSD_GROUPS)]
        ys = []
        for hd in range(SSD_HEADS):
            j = direction * SSD_HEADS + hd
            g = hd // (SSD_HEADS // SSD_GROUPS)
            col = jnp.broadcast_to(acs[:, j:j + 1], (q, q))
            row = acs_t[j:j + 1, :]
            dt_row = dt_t[j:j + 1, :]
            atot = acs_t[j:j + 1, last:last + 1]
            decay = jnp.exp(jnp.where(valid, col - row, NEG_BIG))
            xh = xb[:, hd * pdim:(hd + 1) * pdim]
            h_prev = h_sc[hd]
            c_g = cmat[:, g * n:(g + 1) * n]
            y = (jnp.dot((cb[g] * decay * dt_row).astype(BF16), xh, preferred_element_type=F32)
                 + jnp.dot((c_g * jnp.exp(col[:, :n])).astype(BF16), h_prev.astype(BF16),
                           preferred_element_type=F32))
            if direction == 0:
                y = y + dskip_ref[:, hd * pdim:(hd + 1) * pdim] * xs[:, hd * pdim:(hd + 1) * pdim]
            ys.append(y)
            bw = (bt[g * n:(g + 1) * n, :] * (jnp.exp(atot - row) * dt_row)).astype(BF16)
            h_sc[hd] = jnp.exp(atot) * h_prev + jnp.dot(bw, xh, preferred_element_type=F32)
        y_ref[0, r0:r0 + q, :] = jnp.concatenate(ys, axis=-1)

    subs = range(xbc_ref.shape[1] // q)
    for sub in (subs if direction == 0 else reversed(subs)):
        one_chunk(sub * q)


def _ssd(xbc_act, misc, bias_pad, alog_pad, dskip_wide, direction, n_lat_chunks):
    b, lc, c = xbc_act.shape
    alog_t = jnp.broadcast_to(alog_pad[0, :2 * SSD_HEADS, None], (2 * SSD_HEADS, LANES))
    blk = SSD_STEP_CHUNKS * CHUNK
    assert n_lat_chunks % SSD_STEP_CHUNKS == 0 and lc % blk == 0
    nc = lc // blk
    n_lat_blocks = n_lat_chunks // SSD_STEP_CHUNKS
    if direction == 0:
        cmap = lambda bi, i: (bi, (i + n_lat_blocks) % nc, 0)
    else:
        cmap = lambda bi, i: (bi, nc - 1 - i, 0)
    const2 = lambda bi, i: (0, 0)
    kern = functools.partial(_ssd_kernel, direction=direction)
    return pl.pallas_call(
        kern,
        grid=(b, nc),
        in_specs=[pl.BlockSpec((1, blk, c), cmap),
                  pl.BlockSpec((1, blk, LANES), cmap),
                  pl.BlockSpec((1, LANES), const2),
                  pl.BlockSpec((1, LANES), const2),
                  pl.BlockSpec((2 * SSD_HEADS, LANES), const2),
                  pl.BlockSpec((1, SSD_INNER), const2)],
        out_specs=pl.BlockSpec((1, blk, SSD_INNER), cmap),
        out_shape=jax.ShapeDtypeStruct((b, lc, SSD_INNER), F32),
        scratch_shapes=[pltpu.VMEM((SSD_HEADS, SSD_STATE, SSD_HEAD_DIM), F32)],
        compiler_params=_params(("parallel", "arbitrary")),
        name=f"ssd_dir{direction}",
    )(xbc_act, misc, bias_pad, alog_pad, alog_t, dskip_wide)


def _mixout_kernel(att_ref, y0_ref, y1_ref, z_ref, x_ref, mod_ref, sg_ref, wout_ref, g2_ref,
                   rwh_ref, rwl_ref, rb_ref, xn_ref, h2_ref, gate_ref, cnt_ref):
    half = MLA_HEADS * V_DIM
    nr = x_ref.shape[1] // MIX_SLABS

    def slab(rs):
        y = (y0_ref[0, rs, :] + y1_ref[0, rs, :]) * _silu(z_ref[0, rs, :])
        ssd = _rms(y, sg_ref[...])
        o = (jnp.dot(att_ref[0, rs, :], wout_ref[0:half, :], preferred_element_type=F32)
             + jnp.dot(ssd.astype(BF16), wout_ref[half:, :], preferred_element_type=F32))
        xn = x_ref[0, rs, :] + mod_ref[0, 0, 2:3, :] * o
        xn_ref[0, rs, :] = xn
        h2 = _rms(xn, g2_ref[...]) * (1.0 + mod_ref[0, 0, 4:5, :]) + mod_ref[0, 0, 3:4, :]
        h2_hi = h2.astype(BF16)
        h2_ref[0, rs, :] = h2_hi

        h2_lo = (h2 - h2_hi.astype(F32)).astype(BF16)
        logits = (jnp.dot(h2_hi, rwh_ref[...], preferred_element_type=F32)
                  + jnp.dot(h2_lo, rwh_ref[...], preferred_element_type=F32)
                  + jnp.dot(h2_hi, rwl_ref[...], preferred_element_type=F32)) + rb_ref[...]
        lane = lax.broadcasted_iota(jnp.int32, logits.shape, 1)
        is_grp = jnp.logical_and(lane >= N_EXPERTS, lane < N_EXPERTS + N_GROUPS)
        l1 = jnp.where(is_grp, logits, NEG_BIG)
        m1 = jnp.max(l1, axis=-1, keepdims=True)
        p_top = 1.0 / jnp.sum(jnp.exp(l1 - m1), axis=-1, keepdims=True)
        g_idx = jnp.min(jnp.where(jnp.logical_and(is_grp, l1 == m1), lane - N_EXPERTS, N_GROUPS),
                        axis=-1, keepdims=True)
        sel = jnp.logical_and(lane < N_EXPERTS, (lane // EXPERTS_PER_GROUP) == g_idx)
        l2 = jnp.where(sel, logits, NEG_BIG)
        m2 = jnp.max(l2, axis=-1, keepdims=True)
        i_first = jnp.min(jnp.where(jnp.logical_and(sel, l2 == m2), lane, LANES), axis=-1, keepdims=True)
        l3 = jnp.where(lane == i_first, NEG_BIG, l2)
        m3 = jnp.max(l3, axis=-1, keepdims=True)
        i_second = jnp.min(jnp.where(jnp.logical_and(sel, l3 == m3), lane, LANES), axis=-1, keepdims=True)
        e2 = jnp.exp(m3 - m2)
        inv = p_top / (1.0 + e2)
        gate = jnp.where(lane == i_first, inv, jnp.where(lane == i_second, e2 * inv, 0.0))
        grp_hot = jnp.where(jnp.logical_and(is_grp, lane - N_EXPERTS == g_idx), 1.0, 0.0)
        gate_ref[0, rs, :] = gate + grp_hot
        return jnp.sum(grp_hot, axis=0, keepdims=True)

    counts = [slab(slice(k * nr, (k + 1) * nr)) for k in range(MIX_SLABS)]
    cnt_ref[0, 0] = jnp.broadcast_to(sum(counts), (SUBLANES, LANES))


def _mixout(att, y0, y1, z, xc, modsel, sg, wout, g2, rw, rb, n_lat_tiles):
    rw_hi = rw.astype(BF16)
    rw_lo = (rw - rw_hi.astype(F32)).astype(BF16)
    b, lc, d = xc.shape
    tm = TOKEN_TILE
    nt = lc // tm
    tok = lambda w: pl.BlockSpec((1, tm, w), lambda bi, i: (bi, i, 0))
    const2 = lambda bi, i: (0, 0)
    return pl.pallas_call(
        _mixout_kernel,
        grid=(b, nt),
        in_specs=[tok(SSD_INNER), tok(SSD_INNER), tok(SSD_INNER), tok(SSD_INNER), tok(d),
                  pl.BlockSpec((1, 1, SUBLANES, d), lambda bi, i: (bi, i // n_lat_tiles, 0, 0)),
                  pl.BlockSpec((1, SSD_INNER), const2),
                  pl.BlockSpec(wout.shape, const2),
                  pl.BlockSpec((1, d), const2),
                  pl.BlockSpec((d, LANES), const2),
                  pl.BlockSpec((d, LANES), const2),
                  pl.BlockSpec((1, LANES), const2)],
        out_specs=[tok(d), tok(d), tok(LANES),
                   pl.BlockSpec((1, 1, SUBLANES, LANES), lambda bi, i: (bi, i, 0, 0))],
        out_shape=[jax.ShapeDtypeStruct((b, lc, d), F32),
                   jax.ShapeDtypeStruct((b, lc, d), BF16),
                   jax.ShapeDtypeStruct((b, lc, LANES), F32),
                   jax.ShapeDtypeStruct((b, nt, SUBLANES, LANES), F32)],
        compiler_params=_params(("parallel", "parallel")),
        name="mixout",
    )(att, y0, y1, z, xc, modsel, sg, wout, g2, rw_hi, rw_lo, rb)


def _moe_kernel(off_ref, h2_ref, gate_ref, wgu_ref, wd_ref, y_ref, hs_sc, gs_sc, ys_sc, pt_sc,
                *, tmm, nt, win):
    bi = pl.program_id(0)
    t = pl.program_id(1)
    s = pl.program_id(2)
    base = (bi * nt + t) * SUBLANES
    nblk = tmm // MOE_BLOCK

    @pl.when(s == 0)
    def _():
        gate = gate_ref[0]
        lane = lax.broadcasted_iota(jnp.int32, (tmm, LANES), 1)
        hot = jnp.where(jnp.logical_and(lane >= GRP_LANE, lane < GRP_LANE + N_GROUPS), gate, 0.0)
        r_i = lax.broadcasted_iota(jnp.int32, (MOE_BLOCK, MOE_BLOCK), 0)
        c_i = lax.broadcasted_iota(jnp.int32, (MOE_BLOCK, MOE_BLOCK), 1)
        tri = jnp.where(c_i < r_i, 1.0, 0.0).astype(BF16)
        run = jnp.zeros((1, LANES), F32)
        cums = []
        for blk in range(nblk):
            hb = hot[blk * MOE_BLOCK:(blk + 1) * MOE_BLOCK]
            cums.append(jnp.dot(tri, hb.astype(BF16), preferred_element_type=F32) + run)
            run = run + jnp.sum(hb, axis=0, keepdims=True)
        cum = jnp.concatenate(cums, axis=0)
        lane1 = lax.broadcasted_iota(jnp.int32, (1, LANES), 1)
        offv = jnp.zeros((1, LANES), F32)
        for g in range(N_GROUPS):
            offv = jnp.where(lane1 == GRP_LANE + g, off_ref[base + g].astype(F32), offv)
        pos = jnp.sum(hot * (cum + offv), axis=-1, keepdims=True)
        dst = lax.broadcasted_iota(jnp.int32, (tmm, tmm), 1).astype(F32)
        pt_sc[...] = jnp.where(pos == dst, 1.0, 0.0).astype(BF16)
        pos_row = jnp.broadcast_to(pos, (tmm, LANES)).T[0:1, :]
        src = lax.broadcasted_iota(jnp.int32, (tmm, tmm), 0).astype(F32)
        perm = jnp.where(pos_row == src, 1.0, 0.0).astype(BF16)
        hs_sc[...] = jnp.dot(perm, h2_ref[0], preferred_element_type=F32).astype(BF16)
        g_hi = gate.astype(BF16)
        g_lo = (gate - g_hi.astype(F32)).astype(BF16)
        g2 = jnp.dot(perm, jnp.concatenate([g_hi, g_lo], axis=-1), preferred_element_type=F32)
        gs_sc[...] = g2[:, :LANES] + g2[:, LANES:]
        ys_sc[...] = jnp.zeros(ys_sc.shape, F32)

    grp = s // (EXPERTS_PER_GROUP // MOE_STEP_EXPERTS)
    lo = off_ref[base + grp]
    hi = off_ref[base + grp + 1]
    start = (lo // BF16_ROWS) * BF16_ROWS
    lane_b = lax.broadcasted_iota(jnp.int32, (win, LANES), 1)
    row_b = lax.broadcasted_iota(jnp.int32, (win, 1), 0)

    def window(k, carry):
        first = start + k * win
        r0 = pl.multiple_of(jnp.minimum(first, tmm - win), BF16_ROWS)
        hb = hs_sc[pl.ds(r0, win), :]
        gb = gs_sc[pl.ds(r0, win), :]
        live = r0 + row_b >= first
        acc = jnp.zeros((win, ys_sc.shape[1]), F32)
        for e in range(MOE_STEP_EXPERTS):
            gcol = jnp.sum(jnp.where(lane_b == s * MOE_STEP_EXPERTS + e, gb, 0.0), axis=-1, keepdims=True)
            gcol = jnp.where(live, gcol, 0.0)
            hgu = jnp.dot(hb, wgu_ref[e], preferred_element_type=F32)
            act = (_silu(hgu[:, :EXPERT_FF]) * hgu[:, EXPERT_FF:] * gcol).astype(BF16)
            acc = acc + jnp.dot(act, wd_ref[e], preferred_element_type=F32)
        ys_sc[pl.ds(r0, win), :] += acc
        return carry

    lax.fori_loop(0, (hi - start + win - 1) // win, window, 0)

    @pl.when(s == N_EXPERTS // MOE_STEP_EXPERTS - 1)
    def _():
        y_ref[0] = jnp.dot(pt_sc[...], ys_sc[...].astype(BF16), preferred_element_type=F32).astype(BF16)


def _moe(h2, gate, cnt, wgu, wd, rows, tmm):
    b, _, d = h2.shape
    nt = rows // tmm
    per = tmm // TOKEN_TILE
    n = cnt[:, :nt * per, 0, GRP_LANE:GRP_LANE + N_GROUPS].reshape(b, nt, per, N_GROUPS).sum(axis=2)
    ends = jnp.cumsum(n.astype(jnp.int32), axis=-1)
    zeros = jnp.zeros((b, nt, 1), jnp.int32)
    off = jnp.concatenate([zeros, ends] + [zeros] * (SUBLANES - 1 - N_GROUPS), axis=-1).reshape(-1)
    win = min(tmm, -(-(tmm * 3 // 10) // 64) * 64)
    kern = functools.partial(_moe_kernel, tmm=tmm, nt=nt, win=win)
    tok = lambda w: pl.BlockSpec((1, tmm, w), lambda bi, t, s, o: (bi, t, 0))
    wspec = lambda r, c: pl.BlockSpec((MOE_STEP_EXPERTS, r, c), lambda bi, t, s, o: (s, 0, 0))
    return pl.pallas_call(
        kern,
        grid_spec=pltpu.PrefetchScalarGridSpec(
            num_scalar_prefetch=1,
            grid=(b, nt, N_EXPERTS // MOE_STEP_EXPERTS),
            in_specs=[tok(d), tok(LANES), wspec(d, 2 * EXPERT_FF), wspec(EXPERT_FF, d)],
            out_specs=tok(d),
            scratch_shapes=[pltpu.VMEM((tmm, d), BF16), pltpu.VMEM((tmm, LANES), F32),
                            pltpu.VMEM((tmm, d), F32), pltpu.VMEM((tmm, tmm), BF16)]),
        out_shape=jax.ShapeDtypeStruct((b, rows, d), BF16),
        compiler_params=_params(("parallel", "parallel", "arbitrary"), vmem_mb=56),
        name=f"moe_t{tmm}",
    )(off, h2, gate, wgu, wd)


def _resid_kernel(xn_ref, y_ref, mod_ref, fg_ref, o_ref, *, tm, seq, final):
    row = pl.program_id(1) * tm + lax.broadcasted_iota(jnp.int32, (tm, 1), 0)
    g2 = jnp.where(row >= seq, mod_ref[0, 1, 5:6, :], mod_ref[0, 0, 5:6, :])
    out = xn_ref[0] + g2 * y_ref[0].astype(F32)
    if final:
        out = _rms(out, fg_ref[...])
    o_ref[0] = out


def _resid(xn, y, modsel, fg, rows, seq, final):
    b, _, d = xn.shape
    tm = _pick(rows, (1280, 1024, 768, 512, 256))
    tok = pl.BlockSpec((1, tm, d), lambda bi, i: (bi, i, 0))
    return pl.pallas_call(
        functools.partial(_resid_kernel, tm=tm, seq=seq, final=final),
        grid=(b, rows // tm),
        in_specs=[tok, tok,
                  pl.BlockSpec((1, 2, SUBLANES, d), lambda bi, i: (bi, 0, 0, 0)),
                  pl.BlockSpec((1, d), lambda bi, i: (0, 0))],
        out_specs=tok,
        out_shape=jax.ShapeDtypeStruct((b, rows, d), F32),
        compiler_params=_params(("parallel", "parallel")),
        name="resid_final" if final else "resid",
    )(xn, y, modsel, fg)


def _rope_tables(seq, ctx_len):
    rows = seq // GRID_W
    row = jnp.repeat(jnp.arange(rows), GRID_W)
    col = jnp.tile(jnp.arange(GRID_W), rows)
    inv_freq = ROPE_THETA ** (-jnp.arange(ROPE_DIM // 4, dtype=F32) / (ROPE_DIM // 4))
    ang = jnp.concatenate([row[:, None] * inv_freq, col[:, None] * inv_freq], axis=-1)
    cos, sin = jnp.cos(ang), jnp.sin(ang)
    c32 = jnp.concatenate([cos, cos], axis=-1)
    s32 = jnp.concatenate([-sin, sin], axis=-1)
    c32 = jnp.concatenate([c32, jnp.ones((ctx_len, ROPE_DIM), F32)], axis=0)
    s32 = jnp.concatenate([s32, jnp.zeros((ctx_len, ROPE_DIM), F32)], axis=0)
    pad = ((0, 0), (0, LANES - ROPE_DIM))
    return jnp.concatenate([jnp.pad(c32, pad), jnp.pad(s32, pad)], axis=-1)


def _swap_halves(w):
    lead = w.shape[:-1]
    w4 = w.reshape(lead + (-1, 2, ROPE_DIM // 2))
    return w4[..., ::-1, :].reshape(w.shape)


def kernel(x, c, ctx, c_ctx, w_mod, b_mod, norm1_g, norm2_g, w_in, q_norm_g, w_qb, kv_norm_g, w_kvb,
           conv_w, conv_b, a_log, dt_bias, d_skip, ssd_norm_g, w_out, router_w1, router_b1,
           router_w2, router_b2, w_gate, w_up, w_down, final_g):
    b, seq, d = x.shape
    ctx_len = ctx.shape[1]
    depth = w_mod.shape[0]
    assert ctx_len == TOKEN_TILE and seq % TOKEN_TILE == 0 and seq % GRID_W == 0
    n_lat_tiles = seq // TOKEN_TILE
    n_lat_chunks = seq // CHUNK

    xc = jnp.concatenate([x, ctx], axis=1)
    tab = _rope_tables(seq, ctx_len)

    cvec = jnp.zeros((SUBLANES, d), F32).at[:b].set(c).at[b].set(c_ctx)
    mod = _modulation(cvec, w_mod, b_mod).reshape(depth, SUBLANES, 6, d)

    for l in range(depth):
        last_layer = l == depth - 1
        ml = mod[l]
        modsel = jnp.stack([ml[:b], jnp.broadcast_to(ml[b], (b, 6, d))], axis=1)
        modsel = jnp.pad(modsel, ((0, 0), (0, 0), (0, SUBLANES - 6), (0, 0)))

        w = w_in[l]
        o_kva, o_kr = Q_LORA, Q_LORA + KV_LORA
        o_z = o_kr + ROPE_DIM
        o_xbc, o_dt = o_z + SSD_INNER, o_z + SSD_INNER + XBC_DIM
        w_kr = w[:, o_kr:o_z]
        lane_pad = lambda t: jnp.pad(t, [(0, 0)] * (t.ndim - 1) + [(0, LANES - t.shape[-1])])
        win = jnp.concatenate([w[:, :o_kr], w[:, o_z:o_dt], lane_pad(w[:, o_dt:]), lane_pad(w_kr),
                               lane_pad(_swap_halves(w_kr))], axis=-1).astype(BF16)
        wq3 = w_qb[l].reshape(Q_LORA, MLA_HEADS, NOPE_DIM + ROPE_DIM)
        wq_rope = wq3[:, :, NOPE_DIM:]
        wq = jnp.concatenate([wq3[:, :, :NOPE_DIM].reshape(Q_LORA, MLA_HEADS * NOPE_DIM),
                              lane_pad(wq_rope).reshape(Q_LORA, MLA_HEADS * LANES),
                              lane_pad(_swap_halves(wq_rope)).reshape(Q_LORA, MLA_HEADS * LANES)],
                             axis=-1).astype(BF16)
        wkv3 = w_kvb[l].reshape(KV_LORA, MLA_HEADS, NOPE_DIM + V_DIM)
        wkbt = jnp.transpose(wkv3[:, :, :NOPE_DIM], (1, 2, 0)).astype(BF16)
        wvb = jnp.transpose(wkv3[:, :, NOPE_DIM:], (1, 0, 2)).astype(BF16)

        cw = jnp.pad(conv_w[l], ((0, SUBLANES - CONV_W), (0, 0)))
        q, kv, z, xbc_act, misc = _inproj(xc, modsel, norm1_g[l][None], win, q_norm_g[l][None],
                                          kv_norm_g[l][None], wq, wkbt, tab, cw, conv_b[l][None], n_lat_tiles)
        att = _attention(q, kv, wvb, n_lat_tiles, ctx_len)

        bias_pad = jnp.pad(dt_bias[l].reshape(1, -1), ((0, 0), (0, LANES - 2 * SSD_HEADS)))
        alog_pad = jnp.pad(a_log[l].reshape(1, -1), ((0, 0), (0, LANES - 2 * SSD_HEADS)))
        dskip_wide = jnp.repeat(d_skip[l], SSD_HEAD_DIM)[None]
        y0 = _ssd(xbc_act, misc, bias_pad, alog_pad, dskip_wide, 0, n_lat_chunks)
        y1 = _ssd(xbc_act, misc, bias_pad, alog_pad, dskip_wide, 1, n_lat_chunks)

        rw = jnp.pad(jnp.concatenate([router_w2[l], router_w1[l]], axis=-1),
                     ((0, 0), (0, LANES - N_EXPERTS - N_GROUPS)))
        rb = jnp.pad(jnp.concatenate([router_b2[l], router_b1[l]])[None],
                     ((0, 0), (0, LANES - N_EXPERTS - N_GROUPS)))
        xn, h2, gate, cnt = _mixout(att, y0, y1, z, xc, modsel, ssd_norm_g[l][None], w_out[l].astype(BF16),
                                    norm2_g[l][None], rw, rb, n_lat_tiles)
        rows = seq if last_layer else seq + ctx_len
        tmm = _pick(rows, (1280, 1024, 768, 512, 256))
        wgu = jnp.concatenate([w_gate[l].astype(BF16), w_up[l].astype(BF16)], axis=-1)
        y = _moe(h2, gate, cnt, wgu, w_down[l].astype(BF16), rows, tmm)
        xc = _resid(xn, y, modsel, final_g[None], rows, seq, last_layer)
    return xc
```

```python
import functools

import jax
import jax.numpy as jnp
from jax import lax
from jax.experimental import pallas as pl
from jax.experimental.pallas import tpu as pltpu

F32 = jnp.float32
BF16 = jnp.bfloat16
HIGHEST = lax.Precision.HIGHEST

GRID_W = 64
MLA_HEADS = 8
Q_LORA = 256
KV_LORA = 128
NOPE_DIM = 64
ROPE_DIM = 32
V_DIM = 64
ROPE_THETA = 10000.0
SSD_HEADS = 8
SSD_HEAD_DIM = 64
SSD_INNER = SSD_HEADS * SSD_HEAD_DIM
SSD_GROUPS = 2
SSD_STATE = 64
CONV_W = 5
CHUNK = 128
XBC_DIM = SSD_INNER + 2 * SSD_GROUPS * SSD_STATE
N_GROUPS = 4
EXPERTS_PER_GROUP = 8
N_EXPERTS = N_GROUPS * EXPERTS_PER_GROUP
EXPERT_FF = 256
EPS = 1e-6

LANES = 128
SUBLANES = 8
TOKEN_TILE = 256
KQ_DIM = KV_LORA + ROPE_DIM
ATT_SCALE = (NOPE_DIM + ROPE_DIM) ** -0.5
NEG_BIG = -1e30

COL_QA = 0
COL_KVA = COL_QA + Q_LORA
COL_Z = COL_KVA + KV_LORA
COL_XBC = COL_Z + SSD_INNER
COL_MISC = COL_XBC + XBC_DIM
COL_KR = COL_MISC + LANES
COL_KRP = COL_KR + LANES
IN_COLS_PAD = COL_KRP + LANES
KQ_PAD = 2 * LANES
ONE_LANE = ROPE_DIM
LOG2E = 1.4426950408889634
ATT_CHUNK = 256
ATT_ROW_SPLIT = 2
SSD_STEP_CHUNKS = 2
MIX_SLABS = 1
MOE_BLOCK = 128
BF16_ROWS = 16
MOE_STEP_EXPERTS = 4
GRP_LANE = N_EXPERTS


def _params(sem, vmem_mb=48):
    return pltpu.CompilerParams(dimension_semantics=sem, vmem_limit_bytes=vmem_mb * 1024 * 1024)


def _pick(n, cands):
    for c in cands:
        if n % c == 0:
            return c
    raise ValueError(f"no tile for {n} in {cands}")


def _rms(x, g):
    return x * lax.rsqrt(jnp.mean(x * x, axis=-1, keepdims=True) + EPS) * g


def _silu(x):
    return x * (1.0 / (1.0 + jnp.exp(-x)))


def _lane_tile(x, n):
    return jnp.concatenate([x] * n, axis=1)


def _bdot(a, b):
    return jnp.dot(a.astype(BF16), b.astype(BF16), preferred_element_type=F32)


def _mod_kernel(c_ref, w_ref, b_ref, o_ref):
    s = _silu(c_ref[...])
    o_ref[0] = jnp.dot(s, w_ref[0], precision=HIGHEST, preferred_element_type=F32) + b_ref[0]


def _modulation(cvec, w_mod, b_mod):
    depth, d, n = w_mod.shape
    tn = _pick(n, (768, 512, 256, 128))
    return pl.pallas_call(
        _mod_kernel,
        grid=(depth, n // tn),
        in_specs=[pl.BlockSpec((SUBLANES, d), lambda l, j: (0, 0)),
                  pl.BlockSpec((1, d, tn), lambda l, j: (l, 0, j)),
                  pl.BlockSpec((1, 1, tn), lambda l, j: (l, 0, j))],
        out_specs=pl.BlockSpec((1, SUBLANES, tn), lambda l, j: (l, 0, j)),
        out_shape=jax.ShapeDtypeStruct((depth, SUBLANES, n), F32),
        compiler_params=_params(("parallel", "parallel")),
        name="modulation",
    )(cvec, w_mod, b_mod.reshape(depth, 1, n))


def _inproj_kernel(x_ref, c_ref, xprev_ref, xnext_ref, mod_ref, g_ref, win_ref, qg_ref, kvg_ref, wq_ref,
                   wkbt_ref, tab_ref, cw_ref, cb_ref, q_ref, kv_ref, kvt_ref, z_ref, xbc_ref, misc_ref, ext_sc,
                   *, tm, n_lat_tiles):
    i = pl.program_id(1)
    shift = mod_ref[0, 0, 0:1, :]
    scale = mod_ref[0, 0, 1:2, :]
    modnorm = lambda v: _rms(v, g_ref[...]) * (1.0 + scale) + shift
    x_in = jnp.where(i < n_lat_tiles, x_ref[0], c_ref[0])
    p = _bdot(modnorm(x_in), win_ref[...])
    z_ref[0] = p[:, COL_Z:COL_XBC]
    misc_ref[0] = p[:, COL_MISC:COL_KR]

    halo = SUBLANES
    h_halo = modnorm(jnp.concatenate([xprev_ref[0], xnext_ref[0]], axis=0))
    p_halo = _bdot(h_halo, win_ref[:, COL_XBC:COL_MISC])
    prev_ok = jnp.logical_and(i >= 1, i < n_lat_tiles)
    next_ok = i < n_lat_tiles - 1
    ext_sc[0:halo, :] = jnp.where(prev_ok, p_halo[0:halo], 0.0)
    ext_sc[halo:halo + tm, :] = p[:, COL_XBC:COL_MISC]
    ext_sc[halo + tm:2 * halo + tm, :] = jnp.where(next_ok, p_halo[halo:], 0.0)
    conv = jnp.broadcast_to(cb_ref[...], (tm, XBC_DIM))
    for k in range(CONV_W):
        off = halo - CONV_W // 2 + k
        conv = conv + cw_ref[k:k + 1, :] * ext_sc[off:off + tm, :]
    xbc_ref[0] = _silu(conv)

    rope_c = tab_ref[:, :LANES]
    rope_s = tab_ref[:, LANES:]
    lane = lax.broadcasted_iota(jnp.int32, (1, LANES), 1)
    one_col = jnp.where(lane == ONE_LANE, 1.0, 0.0)

    ckv = _rms(p[:, COL_KVA:COL_Z], kvg_ref[...])
    k_rope = p[:, COL_KR:COL_KRP] * rope_c + p[:, COL_KRP:IN_COLS_PAD] * rope_s + one_col
    kv_tile = jnp.concatenate([ckv, k_rope], axis=-1)
    kv_ref[0] = kv_tile.astype(BF16)
    kvt_ref[0] = kv_tile.T.astype(BF16)

    qn = _rms(p[:, COL_QA:COL_KVA], qg_ref[...])
    q3 = _bdot(qn, wq_ref[...])
    n_nope = MLA_HEADS * NOPE_DIM
    n_grp = MLA_HEADS * LANES
    q_scale = ATT_SCALE * LOG2E
    q_nope = (q3[:, :n_nope] * q_scale).astype(BF16)
    for hd in range(MLA_HEADS):
        q_abs = jnp.dot(q_nope[:, hd * NOPE_DIM:(hd + 1) * NOPE_DIM], wkbt_ref[hd],
                        preferred_element_type=F32)
        lo = n_nope + hd * LANES
        q_rope = (q3[:, lo:lo + LANES] * rope_c + q3[:, lo + n_grp:lo + n_grp + LANES] * rope_s) * q_scale
        q_ref[0, hd] = jnp.concatenate([q_abs, q_rope], axis=-1).astype(BF16)


def _inproj(lat, ctxa, ctx_blk, modsel, g1, win, qg, kvg, wq, wkbt, tab, cw, cb, n_lat_tiles):
    b, _, d = lat.shape
    tm = TOKEN_TILE
    nt = n_lat_tiles + 1
    lc = nt * tm
    hb = tm // SUBLANES
    last = n_lat_tiles * hb - 1
    const2 = lambda bi, i: (0, 0)
    return pl.pallas_call(
        functools.partial(_inproj_kernel, tm=tm, n_lat_tiles=n_lat_tiles),
        grid=(b, nt),
        in_specs=[pl.BlockSpec((1, tm, d), lambda bi, i: (bi, jnp.minimum(i, n_lat_tiles - 1), 0)),
                  pl.BlockSpec((1, tm, d), lambda bi, i: (bi, ctx_blk, 0)),
                  pl.BlockSpec((1, SUBLANES, d), lambda bi, i: (bi, jnp.maximum(i * hb - 1, 0), 0)),
                  pl.BlockSpec((1, SUBLANES, d), lambda bi, i: (bi, jnp.minimum((i + 1) * hb, last), 0)),
                  pl.BlockSpec((1, 1, SUBLANES, d), lambda bi, i: (bi, i // n_lat_tiles, 0, 0)),
                  pl.BlockSpec((1, d), const2),
                  pl.BlockSpec((d, IN_COLS_PAD), const2),
                  pl.BlockSpec((1, Q_LORA), const2),
                  pl.BlockSpec((1, KV_LORA), const2),
                  pl.BlockSpec(wq.shape, const2),
                  pl.BlockSpec(wkbt.shape, lambda bi, i: (0, 0, 0)),
                  pl.BlockSpec((tm, tab.shape[1]), lambda bi, i: (i, 0)),
                  pl.BlockSpec((SUBLANES, XBC_DIM), const2),
                  pl.BlockSpec((1, XBC_DIM), const2)],
        out_specs=[pl.BlockSpec((1, MLA_HEADS, tm, KQ_PAD), lambda bi, i: (bi, 0, i, 0)),
                   pl.BlockSpec((1, tm, KQ_PAD), lambda bi, i: (bi, i, 0)),
                   pl.BlockSpec((1, KQ_PAD, tm), lambda bi, i: (bi, 0, i)),
                   pl.BlockSpec((1, tm, SSD_INNER), lambda bi, i: (bi, i, 0)),
                   pl.BlockSpec((1, tm, XBC_DIM), lambda bi, i: (bi, i, 0)),
                   pl.BlockSpec((1, tm, LANES), lambda bi, i: (bi, i, 0))],
        out_shape=[jax.ShapeDtypeStruct((b, MLA_HEADS, lc, KQ_PAD), BF16),
                   jax.ShapeDtypeStruct((b, lc, KQ_PAD), BF16),
                   jax.ShapeDtypeStruct((b, KQ_PAD, lc), BF16),
                   jax.ShapeDtypeStruct((b, lc, SSD_INNER), F32),
                   jax.ShapeDtypeStruct((b, lc, XBC_DIM), F32),
                   jax.ShapeDtypeStruct((b, lc, LANES), F32)],
        scratch_shapes=[pltpu.VMEM((tm + 2 * SUBLANES, XBC_DIM), F32)],
        compiler_params=_params(("parallel", "parallel")),
        name="inproj",
    )(lat, ctxa, lat, lat, modsel, g1, win, qg, kvg, wq, wkbt, tab, cw, cb)


def _attn_kernel(q_ref, kv_ref, kvt_ref, wvb_ref, o_ref, m_sc, acc_sc,
                 *, tq, tk, tc, row_split, nk, n_lat_q, ctx_len):
    qi = pl.program_id(1)
    kj = pl.program_id(2)
    rows = MLA_HEADS * tq

    @pl.when(kj == 0)
    def _():
        m_sc[...] = jnp.full(m_sc.shape, NEG_BIG, F32)
        acc_sc[...] = jnp.zeros(acc_sc.shape, F32)

    def chunk(q, r0, lo, width):
        nr = q.shape[0]
        kvc = kv_ref[0, lo:lo + width, :]
        s = jnp.dot(q, kvt_ref[0, :, lo:lo + width], preferred_element_type=F32)
        m_prev = m_sc[r0:r0 + nr, :]
        m_new = jnp.maximum(m_prev, jnp.max(s, axis=-1, keepdims=True))
        alpha = jnp.exp2(m_prev - m_new)
        p = jnp.exp2(s - _lane_tile(m_new, width // LANES))
        acc_sc[r0:r0 + nr, :] = (_lane_tile(alpha, KQ_PAD // LANES) * acc_sc[r0:r0 + nr, :]
                                 + jnp.dot(p.astype(BF16), kvc, preferred_element_type=F32))
        m_sc[r0:r0 + nr, :] = m_new

    def step(ctx_only):
        nr = rows // row_split
        qs = [q_ref[0, g * (MLA_HEADS // row_split):(g + 1) * (MLA_HEADS // row_split)].reshape(nr, KQ_PAD)
              for g in range(row_split)]
        spans = ([(tk - ctx_len, ctx_len)] if ctx_only
                 else [(lo, min(tc, tk - lo)) for lo in range(0, tk, tc)])
        for lo, width in spans:
            for g in range(row_split):
                chunk(qs[g], g * nr, lo, width)

    is_ctx_q = qi >= n_lat_q
    pl.when(jnp.logical_not(is_ctx_q))(functools.partial(step, False))
    pl.when(jnp.logical_and(is_ctx_q, kj == nk - 1))(functools.partial(step, True))

    @pl.when(kj == nk - 1)
    def _():
        acc = acc_sc[...]
        denom = acc[:, LANES + ONE_LANE:LANES + ONE_LANE + 1]
        o = (acc[:, :KV_LORA] / denom).astype(BF16)
        outs = [jnp.dot(o[hd * tq:(hd + 1) * tq], wvb_ref[hd], preferred_element_type=F32)
                for hd in range(MLA_HEADS)]
        o_ref[0] = jnp.concatenate(outs, axis=-1).astype(BF16)


def _attention(q, kv, kvt, wvb, n_lat_q, ctx_len):
    b, _, lc, _ = q.shape
    tq = TOKEN_TILE
    tk = _pick(lc, (3328, 1280, 768, 512, 256))
    tc = ATT_CHUNK
    assert ctx_len <= tk and ctx_len == TOKEN_TILE
    nq, nk = lc // tq, lc // tk
    rows = MLA_HEADS * tq
    kern = functools.partial(_attn_kernel, tq=tq, tk=tk, tc=tc, row_split=ATT_ROW_SPLIT, nk=nk, n_lat_q=n_lat_q, ctx_len=ctx_len)
    return pl.pallas_call(
        kern,
        grid=(b, nq, nk),
        in_specs=[pl.BlockSpec((1, MLA_HEADS, tq, KQ_PAD), lambda bi, i, j: (bi, 0, i, 0)),
                  pl.BlockSpec((1, tk, KQ_PAD), lambda bi, i, j: (bi, j, 0)),
                  pl.BlockSpec((1, KQ_PAD, tk), lambda bi, i, j: (bi, 0, j)),
                  pl.BlockSpec(wvb.shape, lambda bi, i, j: (0, 0, 0))],
        out_specs=pl.BlockSpec((1, tq, MLA_HEADS * V_DIM), lambda bi, i, j: (bi, i, 0)),
        out_shape=jax.ShapeDtypeStruct((b, lc, MLA_HEADS * V_DIM), BF16),
        scratch_shapes=[pltpu.VMEM((rows, LANES), F32), pltpu.VMEM((rows, KQ_PAD), F32)],
        compiler_params=_params(("parallel", "parallel", "arbitrary")),
        name="attention",
    )(q, kv, kvt, wvb)


def _ssd_kernel(xbc_ref, misc_ref, bias_ref, alog_ref, alogt_ref, dskip_ref, y_ref, h_sc, *, direction):
    i = pl.program_id(1)
    q = CHUNK
    n = SSD_STATE
    pdim = SSD_HEAD_DIM

    @pl.when(i == 0)
    def _():
        h_sc[...] = jnp.zeros(h_sc.shape, F32)

    softplus = lambda v: jnp.maximum(v, 0.0) + jnp.log(1.0 + jnp.exp(-jnp.abs(v)))
    nd = 2 * SSD_HEADS
    lane = lax.broadcasted_iota(jnp.int32, (1, LANES), 1)
    r_i = lax.broadcasted_iota(jnp.int32, (q, q), 0)
    c_i = lax.broadcasted_iota(jnp.int32, (q, q), 1)
    valid = (c_i <= r_i) if direction == 0 else (c_i >= r_i)
    tri = jnp.where(valid, 1.0, 0.0).astype(F32)
    tri_t = jnp.where(valid, 0.0, 1.0).astype(F32) + jnp.where(r_i == c_i, 1.0, 0.0).astype(F32)
    last = q - 1 if direction == 0 else 0

    def one_chunk(r0):
        xbc = xbc_ref[0, r0:r0 + q, :]
        xs = xbc[:, :SSD_INNER]
        bmat = xbc[:, SSD_INNER:SSD_INNER + SSD_GROUPS * n]
        cmat = xbc[:, SSD_INNER + SSD_GROUPS * n:]
        raw = misc_ref[0, r0:r0 + q, :] + bias_ref[...]
        a = softplus(raw) * jnp.where(lane < nd, -jnp.exp(alog_ref[...]), 0.0)
        dt_t = softplus(raw.T[0:nd, :])
        a_t = dt_t * (-jnp.exp(alogt_ref[...]))
        acs = jnp.dot(tri, a, precision=HIGHEST, preferred_element_type=F32)
        acs_t = jnp.dot(a_t, tri_t, precision=HIGHEST, preferred_element_type=F32)
        bt = bmat.T
        xb = xs.astype(BF16)

        cb = [jnp.dot(cmat[:, g * n:(g + 1) * n].astype(BF16), bt[g * n:(g + 1) * n, :].astype(BF16),
                      preferred_element_type=F32) for g in range(SSD_GROUPS)]
        ys = []
        for hd in range(SSD_HEADS):
            j = direction * SSD_HEADS + hd
            g = hd // (SSD_HEADS // SSD_GROUPS)
            col = jnp.broadcast_to(acs[:, j:j + 1], (q, q))
            row = acs_t[j:j + 1, :]
            dt_row = dt_t[j:j + 1, :]
            atot = acs_t[j:j + 1, last:last + 1]
            decay = jnp.exp(jnp.where(valid, col - row, NEG_BIG))
            xh = xb[:, hd * pdim:(hd + 1) * pdim]
            h_prev = h_sc[hd]
            c_g = cmat[:, g * n:(g + 1) * n]
            y = (jnp.dot((cb[g] * decay * dt_row).astype(BF16), xh, preferred_element_type=F32)
                 + jnp.dot((c_g * jnp.exp(col[:, :n])).astype(BF16), h_prev.astype(BF16),
                           preferred_element_type=F32))
            if direction == 0:
                y = y + dskip_ref[:, hd * pdim:(hd + 1) * pdim] * xs[:, hd * pdim:(hd + 1) * pdim]
            ys.append(y)
            bw = (bt[g * n:(g + 1) * n, :] * (jnp.exp(atot - row) * dt_row)).astype(BF16)
            h_sc[hd] = jnp.exp(atot) * h_prev + jnp.dot(bw, xh, preferred_element_type=F32)
        y_ref[0, r0:r0 + q, :] = jnp.concatenate(ys, axis=-1)

    subs = range(xbc_ref.shape[1] // q)
    for sub in (subs if direction == 0 else reversed(subs)):
        one_chunk(sub * q)


def _ssd(xbc_act, misc, bias_pad, alog_pad, dskip_wide, direction, n_lat_chunks):
    b, lc, c = xbc_act.shape
    alog_t = jnp.broadcast_to(alog_pad[0, :2 * SSD_HEADS, None], (2 * SSD_HEADS, LANES))
    blk = SSD_STEP_CHUNKS * CHUNK
    assert n_lat_chunks % SSD_STEP_CHUNKS == 0 and lc % blk == 0
    nc = lc // blk
    n_lat_blocks = n_lat_chunks // SSD_STEP_CHUNKS
    if direction == 0:
        cmap = lambda bi, i: (bi, (i + n_lat_blocks) % nc, 0)
    else:
        cmap = lambda bi, i: (bi, nc - 1 - i, 0)
    const2 = lambda bi, i: (0, 0)
    kern = functools.partial(_ssd_kernel, direction=direction)
    return pl.pallas_call(
        kern,
        grid=(b, nc),
        in_specs=[pl.BlockSpec((1, blk, c), cmap),
                  pl.BlockSpec((1, blk, LANES), cmap),
                  pl.BlockSpec((1, LANES), const2),
                  pl.BlockSpec((1, LANES), const2),
                  pl.BlockSpec((2 * SSD_HEADS, LANES), const2),
                  pl.BlockSpec((1, SSD_INNER), const2)],
        out_specs=pl.BlockSpec((1, blk, SSD_INNER), cmap),
        out_shape=jax.ShapeDtypeStruct((b, lc, SSD_INNER), F32),
        scratch_shapes=[pltpu.VMEM((SSD_HEADS, SSD_STATE, SSD_HEAD_DIM), F32)],
        compiler_params=_params(("parallel", "arbitrary")),
        name=f"ssd_dir{direction}",
    )(xbc_act, misc, bias_pad, alog_pad, alog_t, dskip_wide)


def _mixout_kernel(att_ref, y0_ref, y1_ref, z_ref, x_ref, c_ref, mod_ref, sg_ref, wout_ref, g2_ref,
                   rwh_ref, rwl_ref, rb_ref, xn_ref, h2_ref, gate_ref, cnt_ref, *, n_lat_tiles):
    half = MLA_HEADS * V_DIM
    nr = x_ref.shape[1] // MIX_SLABS
    is_lat = pl.program_id(1) < n_lat_tiles

    def slab(rs):
        y = (y0_ref[0, rs, :] + y1_ref[0, rs, :]) * _silu(z_ref[0, rs, :])
        ssd = _rms(y, sg_ref[...])
        o = (jnp.dot(att_ref[0, rs, :], wout_ref[0:half, :], preferred_element_type=F32)
             + jnp.dot(ssd.astype(BF16), wout_ref[half:, :], preferred_element_type=F32))
        xn = jnp.where(is_lat, x_ref[0, rs, :], c_ref[0, rs, :]) + mod_ref[0, 0, 2:3, :] * o
        xn_ref[0, rs, :] = xn
        h2 = _rms(xn, g2_ref[...]) * (1.0 + mod_ref[0, 0, 4:5, :]) + mod_ref[0, 0, 3:4, :]
        h2_hi = h2.astype(BF16)
        h2_ref[0, rs, :] = h2_hi

        h2_lo = (h2 - h2_hi.astype(F32)).astype(BF16)
        logits = (jnp.dot(h2_hi, rwh_ref[...], preferred_element_type=F32)
                  + jnp.dot(h2_lo, rwh_ref[...], preferred_element_type=F32)
                  + jnp.dot(h2_hi, rwl_ref[...], preferred_element_type=F32)) + rb_ref[...]
        lane = lax.broadcasted_iota(jnp.int32, logits.shape, 1)
        is_grp = jnp.logical_and(lane >= N_EXPERTS, lane < N_EXPERTS + N_GROUPS)
        l1 = jnp.where(is_grp, logits, NEG_BIG)
        m1 = jnp.max(l1, axis=-1, keepdims=True)
        p_top = 1.0 / jnp.sum(jnp.exp(l1 - m1), axis=-1, keepdims=True)
        g_idx = jnp.min(jnp.where(jnp.logical_and(is_grp, l1 == m1), lane - N_EXPERTS, N_GROUPS),
                        axis=-1, keepdims=True)
        sel = jnp.logical_and(lane < N_EXPERTS, (lane // EXPERTS_PER_GROUP) == g_idx)
        l2 = jnp.where(sel, logits, NEG_BIG)
        m2 = jnp.max(l2, axis=-1, keepdims=True)
        i_first = jnp.min(jnp.where(jnp.logical_and(sel, l2 == m2), lane, LANES), axis=-1, keepdims=True)
        l3 = jnp.where(lane == i_first, NEG_BIG, l2)
        m3 = jnp.max(l3, axis=-1, keepdims=True)
        i_second = jnp.min(jnp.where(jnp.logical_and(sel, l3 == m3), lane, LANES), axis=-1, keepdims=True)
        e2 = jnp.exp(m3 - m2)
        inv = p_top / (1.0 + e2)
        gate = jnp.where(lane == i_first, inv, jnp.where(lane == i_second, e2 * inv, 0.0))
        grp_hot = jnp.where(jnp.logical_and(is_grp, lane - N_EXPERTS == g_idx), 1.0, 0.0)
        gate_ref[0, rs, :] = gate + grp_hot
        return jnp.sum(grp_hot, axis=0, keepdims=True)

    counts = [slab(slice(k * nr, (k + 1) * nr)) for k in range(MIX_SLABS)]
    cnt_ref[0, 0] = jnp.broadcast_to(sum(counts), (SUBLANES, LANES))


def _mixout(att, y0, y1, z, lat, ctxa, ctx_blk, modsel, sg, wout, g2, rw, rb, n_lat_tiles):
    rw_hi = rw.astype(BF16)
    rw_lo = (rw - rw_hi.astype(F32)).astype(BF16)
    b, _, d = lat.shape
    tm = TOKEN_TILE
    nt = n_lat_tiles + 1
    lc = nt * tm
    tok = lambda w: pl.BlockSpec((1, tm, w), lambda bi, i: (bi, i, 0))
    const2 = lambda bi, i: (0, 0)
    return pl.pallas_call(
        functools.partial(_mixout_kernel, n_lat_tiles=n_lat_tiles),
        grid=(b, nt),
        in_specs=[tok(SSD_INNER), tok(SSD_INNER), tok(SSD_INNER), tok(SSD_INNER),
                  pl.BlockSpec((1, tm, d), lambda bi, i: (bi, jnp.minimum(i, n_lat_tiles - 1), 0)),
                  pl.BlockSpec((1, tm, d), lambda bi, i: (bi, ctx_blk, 0)),
                  pl.BlockSpec((1, 1, SUBLANES, d), lambda bi, i: (bi, i // n_lat_tiles, 0, 0)),
                  pl.BlockSpec((1, SSD_INNER), const2),
                  pl.BlockSpec(wout.shape, const2),
                  pl.BlockSpec((1, d), const2),
                  pl.BlockSpec((d, LANES), const2),
                  pl.BlockSpec((d, LANES), const2),
                  pl.BlockSpec((1, LANES), const2)],
        out_specs=[tok(d), tok(d), tok(LANES),
                   pl.BlockSpec((1, 1, SUBLANES, LANES), lambda bi, i: (bi, i, 0, 0))],
        out_shape=[jax.ShapeDtypeStruct((b, lc, d), F32),
                   jax.ShapeDtypeStruct((b, lc, d), BF16),
                   jax.ShapeDtypeStruct((b, lc, LANES), F32),
                   jax.ShapeDtypeStruct((b, nt, SUBLANES, LANES), F32)],
        compiler_params=_params(("parallel", "parallel")),
        name="mixout",
    )(att, y0, y1, z, lat, ctxa, modsel, sg, wout, g2, rw_hi, rw_lo, rb)


def _moe_kernel(off_ref, h2_ref, gate_ref, wg_ref, wu_ref, wd_ref, y_ref, hs_sc, gs_sc, ys_sc, pt_sc,
                *, tmm, nt, win):
    bi = pl.program_id(0)
    t = pl.program_id(1)
    s = pl.program_id(2)
    base = (bi * nt + t) * SUBLANES
    nblk = tmm // MOE_BLOCK

    @pl.when(s == 0)
    def _():
        gate = gate_ref[0]
        lane = lax.broadcasted_iota(jnp.int32, (tmm, LANES), 1)
        hot = jnp.where(jnp.logical_and(lane >= GRP_LANE, lane < GRP_LANE + N_GROUPS), gate, 0.0)
        r_i = lax.broadcasted_iota(jnp.int32, (MOE_BLOCK, MOE_BLOCK), 0)
        c_i = lax.broadcasted_iota(jnp.int32, (MOE_BLOCK, MOE_BLOCK), 1)
        tri = jnp.where(c_i < r_i, 1.0, 0.0).astype(BF16)
        run = jnp.zeros((1, LANES), F32)
        cums = []
        for blk in range(nblk):
            hb = hot[blk * MOE_BLOCK:(blk + 1) * MOE_BLOCK]
            cums.append(jnp.dot(tri, hb.astype(BF16), preferred_element_type=F32) + run)
            run = run + jnp.sum(hb, axis=0, keepdims=True)
        cum = jnp.concatenate(cums, axis=0)
        lane1 = lax.broadcasted_iota(jnp.int32, (1, LANES), 1)
        offv = jnp.zeros((1, LANES), F32)
        for g in range(N_GROUPS):
            offv = jnp.where(lane1 == GRP_LANE + g, off_ref[base + g].astype(F32), offv)
        pos = jnp.sum(hot * (cum + offv), axis=-1, keepdims=True)
        dst = lax.broadcasted_iota(jnp.int32, (tmm, tmm), 1).astype(F32)
        pt_sc[...] = jnp.where(pos == dst, 1.0, 0.0).astype(BF16)
        pos_row = jnp.broadcast_to(pos, (tmm, LANES)).T[0:1, :]
        src = lax.broadcasted_iota(jnp.int32, (tmm, tmm), 0).astype(F32)
        perm = jnp.where(pos_row == src, 1.0, 0.0).astype(BF16)
        hs_sc[...] = jnp.dot(perm, h2_ref[0], preferred_element_type=F32).astype(BF16)
        g_hi = gate.astype(BF16)
        g_lo = (gate - g_hi.astype(F32)).astype(BF16)
        g2 = jnp.dot(perm, jnp.concatenate([g_hi, g_lo], axis=-1), preferred_element_type=F32)
        gs_sc[...] = g2[:, :LANES] + g2[:, LANES:]
        ys_sc[...] = jnp.zeros(ys_sc.shape, F32)

    grp = s // (EXPERTS_PER_GROUP // MOE_STEP_EXPERTS)
    lo = off_ref[base + grp]
    hi = off_ref[base + grp + 1]
    start = (lo // BF16_ROWS) * BF16_ROWS
    lane_b = lax.broadcasted_iota(jnp.int32, (win, LANES), 1)
    row_b = lax.broadcasted_iota(jnp.int32, (win, 1), 0)

    def window(k, carry):
        first = start + k * win
        r0 = pl.multiple_of(jnp.minimum(first, tmm - win), BF16_ROWS)
        hb = hs_sc[pl.ds(r0, win), :]
        gb = gs_sc[pl.ds(r0, win), :]
        live = r0 + row_b >= first
        acc = jnp.zeros((win, ys_sc.shape[1]), F32)
        for e in range(MOE_STEP_EXPERTS):
            gcol = jnp.sum(jnp.where(lane_b == s * MOE_STEP_EXPERTS + e, gb, 0.0), axis=-1, keepdims=True)
            gcol = jnp.where(live, gcol, 0.0)
            hg = jnp.dot(hb, wg_ref[e], preferred_element_type=F32)
            hu = jnp.dot(hb, wu_ref[e], preferred_element_type=F32)
            act = (_silu(hg) * hu * gcol).astype(BF16)
            acc = acc + jnp.dot(act, wd_ref[e], preferred_element_type=F32)
        ys_sc[pl.ds(r0, win), :] += acc
        return carry

    lax.fori_loop(0, (hi - start + win - 1) // win, window, 0)

    @pl.when(s == N_EXPERTS // MOE_STEP_EXPERTS - 1)
    def _():
        y_ref[0] = jnp.dot(pt_sc[...], ys_sc[...].astype(BF16), preferred_element_type=F32).astype(BF16)


def _moe(h2, gate, cnt, wg, wu, wd, rows, tmm):
    b, _, d = h2.shape
    nt = rows // tmm
    per = tmm // TOKEN_TILE
    n = cnt[:, :nt * per, 0, GRP_LANE:GRP_LANE + N_GROUPS].reshape(b, nt, per, N_GROUPS).sum(axis=2)
    ends = jnp.cumsum(n.astype(jnp.int32), axis=-1)
    zeros = jnp.zeros((b, nt, 1), jnp.int32)
    off = jnp.concatenate([zeros, ends] + [zeros] * (SUBLANES - 1 - N_GROUPS), axis=-1).reshape(-1)
    win = min(tmm, -(-(tmm * 3 // 10) // 64) * 64)
    kern = functools.partial(_moe_kernel, tmm=tmm, nt=nt, win=win)
    tok = lambda w: pl.BlockSpec((1, tmm, w), lambda bi, t, s, o: (bi, t, 0))
    wspec = lambda r, c: pl.BlockSpec((MOE_STEP_EXPERTS, r, c), lambda bi, t, s, o: (s, 0, 0))
    return pl.pallas_call(
        kern,
        grid_spec=pltpu.PrefetchScalarGridSpec(
            num_scalar_prefetch=1,
            grid=(b, nt, N_EXPERTS // MOE_STEP_EXPERTS),
            in_specs=[tok(d), tok(LANES), wspec(d, EXPERT_FF), wspec(d, EXPERT_FF), wspec(EXPERT_FF, d)],
            out_specs=tok(d),
            scratch_shapes=[pltpu.VMEM((tmm, d), BF16), pltpu.VMEM((tmm, LANES), F32),
                            pltpu.VMEM((tmm, d), F32), pltpu.VMEM((tmm, tmm), BF16)]),
        out_shape=jax.ShapeDtypeStruct((b, rows, d), BF16),
        compiler_params=_params(("parallel", "parallel", "arbitrary"), vmem_mb=56),
        name=f"moe_t{tmm}",
    )(off, h2, gate, wg, wu, wd)


def _resid_kernel(xn_ref, y_ref, mod_ref, fg_ref, o_ref, *, tm, seq, final):
    row = pl.program_id(1) * tm + lax.broadcasted_iota(jnp.int32, (tm, 1), 0)
    g2 = jnp.where(row >= seq, mod_ref[0, 1, 5:6, :], mod_ref[0, 0, 5:6, :])
    out = xn_ref[0] + g2 * y_ref[0].astype(F32)
    if final:
        out = _rms(out, fg_ref[...])
    o_ref[0] = out


def _resid(xn, y, modsel, fg, rows, seq, final):
    b, _, d = xn.shape
    tm = _pick(rows, (1280, 1024, 768, 512, 256))
    tok = pl.BlockSpec((1, tm, d), lambda bi, i: (bi, i, 0))
    return pl.pallas_call(
        functools.partial(_resid_kernel, tm=tm, seq=seq, final=final),
        grid=(b, rows // tm),
        in_specs=[tok, tok,
                  pl.BlockSpec((1, 2, SUBLANES, d), lambda bi, i: (bi, 0, 0, 0)),
                  pl.BlockSpec((1, d), lambda bi, i: (0, 0))],
        out_specs=tok,
        out_shape=jax.ShapeDtypeStruct((b, rows, d), F32),
        compiler_params=_params(("parallel", "parallel")),
        name="resid_final" if final else "resid",
    )(xn, y, modsel, fg)


def _rope_tables(seq, ctx_len):
    rows = seq // GRID_W
    row = jnp.repeat(jnp.arange(rows), GRID_W)
    col = jnp.tile(jnp.arange(GRID_W), rows)
    inv_freq = ROPE_THETA ** (-jnp.arange(ROPE_DIM // 4, dtype=F32) / (ROPE_DIM // 4))
    ang = jnp.concatenate([row[:, None] * inv_freq, col[:, None] * inv_freq], axis=-1)
    cos, sin = jnp.cos(ang), jnp.sin(ang)
    c32 = jnp.concatenate([cos, cos], axis=-1)
    s32 = jnp.concatenate([-sin, sin], axis=-1)
    c32 = jnp.concatenate([c32, jnp.ones((ctx_len, ROPE_DIM), F32)], axis=0)
    s32 = jnp.concatenate([s32, jnp.zeros((ctx_len, ROPE_DIM), F32)], axis=0)
    pad = ((0, 0), (0, LANES - ROPE_DIM))
    return jnp.concatenate([jnp.pad(c32, pad), jnp.pad(s32, pad)], axis=-1)


def _swap_halves(w):
    lead = w.shape[:-1]
    w4 = w.reshape(lead + (-1, 2, ROPE_DIM // 2))
    return w4[..., ::-1, :].reshape(w.shape)


def kernel(x, c, ctx, c_ctx, w_mod, b_mod, norm1_g, norm2_g, w_in, q_norm_g, w_qb, kv_norm_g, w_kvb,
           conv_w, conv_b, a_log, dt_bias, d_skip, ssd_norm_g, w_out, router_w1, router_b1,
           router_w2, router_b2, w_gate, w_up, w_down, final_g):
    b, seq, d = x.shape
    ctx_len = ctx.shape[1]
    depth = w_mod.shape[0]
    assert ctx_len == TOKEN_TILE and seq % TOKEN_TILE == 0 and seq % GRID_W == 0
    n_lat_tiles = seq // TOKEN_TILE
    n_lat_chunks = seq // CHUNK

    lat, ctxa, ctx_blk = x, ctx, 0
    tab = _rope_tables(seq, ctx_len)

    cvec = jnp.zeros((SUBLANES, d), F32).at[:b].set(c).at[b].set(c_ctx)
    mod = _modulation(cvec, w_mod, b_mod).reshape(depth, SUBLANES, 6, d)

    for l in range(depth):
        last_layer = l == depth - 1
        ml = mod[l]
        modsel = jnp.stack([ml[:b], jnp.broadcast_to(ml[b], (b, 6, d))], axis=1)
        modsel = jnp.pad(modsel, ((0, 0), (0, 0), (0, SUBLANES - 6), (0, 0)))

        w = w_in[l]
        o_kva, o_kr = Q_LORA, Q_LORA + KV_LORA
        o_z = o_kr + ROPE_DIM
        o_xbc, o_dt = o_z + SSD_INNER, o_z + SSD_INNER + XBC_DIM
        w_kr = w[:, o_kr:o_z]
        lane_pad = lambda t: jnp.pad(t, [(0, 0)] * (t.ndim - 1) + [(0, LANES - t.shape[-1])])
        win = jnp.concatenate([w[:, :o_kr], w[:, o_z:o_dt], lane_pad(w[:, o_dt:]), lane_pad(w_kr),
                               lane_pad(_swap_halves(w_kr))], axis=-1).astype(BF16)
        wq3 = w_qb[l].reshape(Q_LORA, MLA_HEADS, NOPE_DIM + ROPE_DIM)
        wq_rope = wq3[:, :, NOPE_DIM:]
        wq = jnp.concatenate([wq3[:, :, :NOPE_DIM].reshape(Q_LORA, MLA_HEADS * NOPE_DIM),
                              lane_pad(wq_rope).reshape(Q_LORA, MLA_HEADS * LANES),
                              lane_pad(_swap_halves(wq_rope)).reshape(Q_LORA, MLA_HEADS * LANES)],
                             axis=-1).astype(BF16)
        wkv3 = w_kvb[l].reshape(KV_LORA, MLA_HEADS, NOPE_DIM + V_DIM)
        wkbt = jnp.transpose(wkv3[:, :, :NOPE_DIM], (1, 2, 0)).astype(BF16)
        wvb = jnp.transpose(wkv3[:, :, NOPE_DIM:], (1, 0, 2)).astype(BF16)

        cw = jnp.pad(conv_w[l], ((0, SUBLANES - CONV_W), (0, 0)))
        q, kv, kvt, z, xbc_act, misc = _inproj(lat, ctxa, ctx_blk, modsel, norm1_g[l][None], win, q_norm_g[l][None],
                                               kv_norm_g[l][None], wq, wkbt, tab, cw, conv_b[l][None],
                                               n_lat_tiles)
        att = _attention(q, kv, kvt, wvb, n_lat_tiles, ctx_len)

        bias_pad = jnp.pad(dt_bias[l].reshape(1, -1), ((0, 0), (0, LANES - 2 * SSD_HEADS)))
        alog_pad = jnp.pad(a_log[l].reshape(1, -1), ((0, 0), (0, LANES - 2 * SSD_HEADS)))
        dskip_wide = jnp.repeat(d_skip[l], SSD_HEAD_DIM)[None]
        y0 = _ssd(xbc_act, misc, bias_pad, alog_pad, dskip_wide, 0, n_lat_chunks)
        y1 = _ssd(xbc_act, misc, bias_pad, alog_pad, dskip_wide, 1, n_lat_chunks)

        rw = jnp.pad(jnp.concatenate([router_w2[l], router_w1[l]], axis=-1),
                     ((0, 0), (0, LANES - N_EXPERTS - N_GROUPS)))
        rb = jnp.pad(jnp.concatenate([router_b2[l], router_b1[l]])[None],
                     ((0, 0), (0, LANES - N_EXPERTS - N_GROUPS)))
        xn, h2, gate, cnt = _mixout(att, y0, y1, z, lat, ctxa, ctx_blk, modsel, ssd_norm_g[l][None], w_out[l].astype(BF16),
                                    norm2_g[l][None], rw, rb, n_lat_tiles)
        rows = seq if last_layer else seq + ctx_len
        tmm = _pick(rows, (1280, 1024, 768, 512, 256))
        y = _moe(h2, gate, cnt, w_gate[l].astype(BF16), w_up[l].astype(BF16), w_down[l].astype(BF16),
                 rows, tmm)
        lat = _resid(xn, y, modsel, final_g[None], rows, seq, last_layer)
        ctxa, ctx_blk = lat, n_lat_tiles
    return lat
```

```python
import functools

import jax
import jax.numpy as jnp
from jax import lax
from jax.experimental import pallas as pl
from jax.experimental.pallas import tpu as pltpu

F32 = jnp.float32
BF16 = jnp.bfloat16
HIGHEST = lax.Precision.HIGHEST

GRID_W = 64
MLA_HEADS = 8
Q_LORA = 256
KV_LORA = 128
NOPE_DIM = 64
ROPE_DIM = 32
V_DIM = 64
ROPE_THETA = 10000.0
SSD_HEADS = 8
SSD_HEAD_DIM = 64
SSD_INNER = SSD_HEADS * SSD_HEAD_DIM
SSD_GROUPS = 2
SSD_STATE = 64
CONV_W = 5
CHUNK = 128
XBC_DIM = SSD_INNER + 2 * SSD_GROUPS * SSD_STATE
N_GROUPS = 4
EXPERTS_PER_GROUP = 8
N_EXPERTS = N_GROUPS * EXPERTS_PER_GROUP
EXPERT_FF = 256
EPS = 1e-6

LANES = 128
SUBLANES = 8
TOKEN_TILE = 256
KQ_DIM = KV_LORA + ROPE_DIM
ATT_SCALE = (NOPE_DIM + ROPE_DIM) ** -0.5
NEG_BIG = -1e30

COL_QA = 0
COL_KVA = COL_QA + Q_LORA
COL_Z = COL_KVA + KV_LORA
COL_XBC = COL_Z + SSD_INNER
COL_MISC = COL_XBC + XBC_DIM
COL_KR = COL_MISC + LANES
COL_KRP = COL_KR + LANES
IN_COLS_PAD = COL_KRP + LANES
KQ_PAD = 2 * LANES
ONE_LANE = ROPE_DIM
LOG2E = 1.4426950408889634
ATT_CHUNK = 256
ATT_Q_TILE = 512
ATT_GROUP_ROWS = 1024
SSD_STEP_CHUNKS = 2
MIX_SLABS = 1
MOE_BLOCK = 128
BF16_ROWS = 16
MOE_STEP_EXPERTS = 4
GRP_LANE = N_EXPERTS


def _params(sem, vmem_mb=48):
    return pltpu.CompilerParams(dimension_semantics=sem, vmem_limit_bytes=vmem_mb * 1024 * 1024)


def _pick(n, cands):
    for c in cands:
        if n % c == 0:
            return c
    raise ValueError(f"no tile for {n} in {cands}")


def _rms(x, g):
    return x * lax.rsqrt(jnp.mean(x * x, axis=-1, keepdims=True) + EPS) * g


def _silu(x):
    return x * (1.0 / (1.0 + jnp.exp(-x)))


def _lane_tile(x, n):
    return jnp.concatenate([x] * n, axis=1)


def _bdot(a, b):
    return jnp.dot(a.astype(BF16), b.astype(BF16), preferred_element_type=F32)


def _mod_kernel(c_ref, w_ref, b_ref, o_ref):
    s = _silu(c_ref[...])
    o_ref[0] = jnp.dot(s, w_ref[0], precision=HIGHEST, preferred_element_type=F32) + b_ref[0]


def _modulation(cvec, w_mod, b_mod):
    depth, d, n = w_mod.shape
    tn = _pick(n, (768, 512, 256, 128))
    return pl.pallas_call(
        _mod_kernel,
        grid=(depth, n // tn),
        in_specs=[pl.BlockSpec((SUBLANES, d), lambda l, j: (0, 0)),
                  pl.BlockSpec((1, d, tn), lambda l, j: (l, 0, j)),
                  pl.BlockSpec((1, 1, tn), lambda l, j: (l, 0, j))],
        out_specs=pl.BlockSpec((1, SUBLANES, tn), lambda l, j: (l, 0, j)),
        out_shape=jax.ShapeDtypeStruct((depth, SUBLANES, n), F32),
        compiler_params=_params(("parallel", "parallel")),
        name="modulation",
    )(cvec, w_mod, b_mod.reshape(depth, 1, n))


def _inproj_kernel(x_ref, c_ref, xprev_ref, xnext_ref, mod_ref, g_ref, win_ref, qg_ref, kvg_ref, wq_ref,
                   wkbt_ref, tab_ref, cw_ref, cb_ref, q_ref, kv_ref, kvt_ref, z_ref, xbc_ref, misc_ref, ext_sc,
                   *, tm, n_lat_tiles):
    i = pl.program_id(1)
    shift = mod_ref[0, 0, 0:1, :]
    scale = mod_ref[0, 0, 1:2, :]
    modnorm = lambda v: _rms(v, g_ref[...]) * (1.0 + scale) + shift
    x_in = jnp.where(i < n_lat_tiles, x_ref[0], c_ref[0])
    p = _bdot(modnorm(x_in), win_ref[...])
    z_ref[0] = p[:, COL_Z:COL_XBC]
    misc_ref[0] = p[:, COL_MISC:COL_KR]

    halo = SUBLANES
    h_halo = modnorm(jnp.concatenate([xprev_ref[0], xnext_ref[0]], axis=0))
    p_halo = _bdot(h_halo, win_ref[:, COL_XBC:COL_MISC])
    prev_ok = jnp.logical_and(i >= 1, i < n_lat_tiles)
    next_ok = i < n_lat_tiles - 1
    ext_sc[0:halo, :] = jnp.where(prev_ok, p_halo[0:halo], 0.0)
    ext_sc[halo:halo + tm, :] = p[:, COL_XBC:COL_MISC]
    ext_sc[halo + tm:2 * halo + tm, :] = jnp.where(next_ok, p_halo[halo:], 0.0)
    conv = jnp.broadcast_to(cb_ref[...], (tm, XBC_DIM))
    for k in range(CONV_W):
        off = halo - CONV_W // 2 + k
        conv = conv + cw_ref[k:k + 1, :] * ext_sc[off:off + tm, :]
    xbc_ref[0] = _silu(conv)

    rope_c = tab_ref[:, :LANES]
    rope_s = tab_ref[:, LANES:]
    lane = lax.broadcasted_iota(jnp.int32, (1, LANES), 1)
    one_col = jnp.where(lane == ONE_LANE, 1.0, 0.0)

    ckv = _rms(p[:, COL_KVA:COL_Z], kvg_ref[...])
    k_rope = p[:, COL_KR:COL_KRP] * rope_c + p[:, COL_KRP:IN_COLS_PAD] * rope_s + one_col
    kv_tile = jnp.concatenate([ckv, k_rope], axis=-1)
    kv_ref[0] = kv_tile.astype(BF16)
    kvt_ref[0] = kv_tile.T.astype(BF16)

    qn = _rms(p[:, COL_QA:COL_KVA], qg_ref[...])
    q3 = _bdot(qn, wq_ref[...])
    n_nope = MLA_HEADS * NOPE_DIM
    n_grp = MLA_HEADS * LANES
    q_scale = ATT_SCALE * LOG2E
    q_nope = (q3[:, :n_nope] * q_scale).astype(BF16)
    for hd in range(MLA_HEADS):
        q_abs = jnp.dot(q_nope[:, hd * NOPE_DIM:(hd + 1) * NOPE_DIM], wkbt_ref[hd],
                        preferred_element_type=F32)
        lo = n_nope + hd * LANES
        q_rope = (q3[:, lo:lo + LANES] * rope_c + q3[:, lo + n_grp:lo + n_grp + LANES] * rope_s) * q_scale
        q_ref[0, hd] = jnp.concatenate([q_abs, q_rope], axis=-1).astype(BF16)


def _inproj(lat, ctxa, ctx_blk, modsel, g1, win, qg, kvg, wq, wkbt, tab, cw, cb, n_lat_tiles):
    b, _, d = lat.shape
    tm = TOKEN_TILE
    nt = n_lat_tiles + 1
    lc = nt * tm
    hb = tm // SUBLANES
    last = n_lat_tiles * hb - 1
    const2 = lambda bi, i: (0, 0)
    return pl.pallas_call(
        functools.partial(_inproj_kernel, tm=tm, n_lat_tiles=n_lat_tiles),
        grid=(b, nt),
        in_specs=[pl.BlockSpec((1, tm, d), lambda bi, i: (bi, jnp.minimum(i, n_lat_tiles - 1), 0)),
                  pl.BlockSpec((1, tm, d), lambda bi, i: (bi, ctx_blk, 0)),
                  pl.BlockSpec((1, SUBLANES, d), lambda bi, i: (bi, jnp.maximum(i * hb - 1, 0), 0)),
                  pl.BlockSpec((1, SUBLANES, d), lambda bi, i: (bi, jnp.minimum((i + 1) * hb, last), 0)),
                  pl.BlockSpec((1, 1, SUBLANES, d), lambda bi, i: (bi, i // n_lat_tiles, 0, 0)),
                  pl.BlockSpec((1, d), const2),
                  pl.BlockSpec((d, IN_COLS_PAD), const2),
                  pl.BlockSpec((1, Q_LORA), const2),
                  pl.BlockSpec((1, KV_LORA), const2),
                  pl.BlockSpec(wq.shape, const2),
                  pl.BlockSpec(wkbt.shape, lambda bi, i: (0, 0, 0)),
                  pl.BlockSpec((tm, tab.shape[1]), lambda bi, i: (i, 0)),
                  pl.BlockSpec((SUBLANES, XBC_DIM), const2),
                  pl.BlockSpec((1, XBC_DIM), const2)],
        out_specs=[pl.BlockSpec((1, MLA_HEADS, tm, KQ_PAD), lambda bi, i: (bi, 0, i, 0)),
                   pl.BlockSpec((1, tm, KQ_PAD), lambda bi, i: (bi, i, 0)),
                   pl.BlockSpec((1, KQ_PAD, tm), lambda bi, i: (bi, 0, i)),
                   pl.BlockSpec((1, tm, SSD_INNER), lambda bi, i: (bi, i, 0)),
                   pl.BlockSpec((1, tm, XBC_DIM), lambda bi, i: (bi, i, 0)),
                   pl.BlockSpec((1, tm, LANES), lambda bi, i: (bi, i, 0))],
        out_shape=[jax.ShapeDtypeStruct((b, MLA_HEADS, lc, KQ_PAD), BF16),
                   jax.ShapeDtypeStruct((b, lc, KQ_PAD), BF16),
                   jax.ShapeDtypeStruct((b, KQ_PAD, lc), BF16),
                   jax.ShapeDtypeStruct((b, lc, SSD_INNER), F32),
                   jax.ShapeDtypeStruct((b, lc, XBC_DIM), F32),
                   jax.ShapeDtypeStruct((b, lc, LANES), F32)],
        scratch_shapes=[pltpu.VMEM((tm + 2 * SUBLANES, XBC_DIM), F32)],
        compiler_params=_params(("parallel", "parallel")),
        name="inproj",
    )(lat, ctxa, lat, lat, modsel, g1, win, qg, kvg, wq, wkbt, tab, cw, cb)


def _attn_kernel(q_ref, kv_ref, kvt_ref, wvb_ref, o_ref, m_sc, acc_sc,
                 *, tq, tk, tc, row_split, nk):
    kj = pl.program_id(2)
    rows = MLA_HEADS * tq

    @pl.when(kj == 0)
    def _():
        m_sc[...] = jnp.full(m_sc.shape, NEG_BIG, F32)
        acc_sc[...] = jnp.zeros(acc_sc.shape, F32)

    def chunk(q, r0, lo, width):
        nr = q.shape[0]
        kvc = kv_ref[0, lo:lo + width, :]
        s = jnp.dot(q, kvt_ref[0, :, lo:lo + width], preferred_element_type=F32)
        m_prev = m_sc[r0:r0 + nr, :]
        m_new = jnp.maximum(m_prev, jnp.max(s, axis=-1, keepdims=True))
        alpha = jnp.exp2(m_prev - m_new)
        p = jnp.exp2(s - _lane_tile(m_new, width // LANES))
        acc_sc[r0:r0 + nr, :] = (_lane_tile(alpha, KQ_PAD // LANES) * acc_sc[r0:r0 + nr, :]
                                 + jnp.dot(p.astype(BF16), kvc, preferred_element_type=F32))
        m_sc[r0:r0 + nr, :] = m_new

    nr = rows // row_split
    hpg = MLA_HEADS // row_split
    qs = [q_ref[0, g * hpg:(g + 1) * hpg].reshape(nr, KQ_PAD) for g in range(row_split)]
    for lo in range(0, tk, tc):
        for g in range(row_split):
            chunk(qs[g], g * nr, lo, min(tc, tk - lo))

    @pl.when(kj == nk - 1)
    def _():
        acc = acc_sc[...]
        denom = acc[:, LANES + ONE_LANE:LANES + ONE_LANE + 1]
        o = (acc[:, :KV_LORA] / denom).astype(BF16)
        outs = [jnp.dot(o[hd * tq:(hd + 1) * tq], wvb_ref[hd], preferred_element_type=F32)
                for hd in range(MLA_HEADS)]
        o_ref[0] = jnp.concatenate(outs, axis=-1).astype(BF16)


def _attention_call(q, kv, kvt, wvb, *, tq, tk, q0, nq, k0, nk, row_split, name):
    b = q.shape[0]
    rows = MLA_HEADS * tq
    kern = functools.partial(_attn_kernel, tq=tq, tk=tk, tc=ATT_CHUNK, row_split=row_split, nk=nk)
    return pl.pallas_call(
        kern,
        grid=(b, nq, nk),
        in_specs=[pl.BlockSpec((1, MLA_HEADS, tq, KQ_PAD), lambda bi, i, j: (bi, 0, q0 + i, 0)),
                  pl.BlockSpec((1, tk, KQ_PAD), lambda bi, i, j: (bi, k0 + j, 0)),
                  pl.BlockSpec((1, KQ_PAD, tk), lambda bi, i, j: (bi, 0, k0 + j)),
                  pl.BlockSpec(wvb.shape, lambda bi, i, j: (0, 0, 0))],
        out_specs=pl.BlockSpec((1, tq, MLA_HEADS * V_DIM), lambda bi, i, j: (bi, i, 0)),
        out_shape=jax.ShapeDtypeStruct((b, nq * tq, MLA_HEADS * V_DIM), BF16),
        scratch_shapes=[pltpu.VMEM((rows, LANES), F32), pltpu.VMEM((rows, KQ_PAD), F32)],
        compiler_params=_params(("parallel", "parallel", "arbitrary")),
        name=name,
    )(q, kv, kvt, wvb)


def _attention(q, kv, kvt, wvb, seq, ctx_len, ctx_out):
    lc = seq + ctx_len
    tq = _pick(seq, (ATT_Q_TILE, TOKEN_TILE))
    tk = _pick(lc, (3328, 1280, 768, 512, 256))
    att = _attention_call(q, kv, kvt, wvb, tq=tq, tk=tk, q0=0, nq=seq // tq, k0=0, nk=lc // tk,
                          row_split=MLA_HEADS * tq // ATT_GROUP_ROWS, name="attention")
    if not ctx_out:
        return att
    att_ctx = _attention_call(q, kv, kvt, wvb, tq=ctx_len, tk=ctx_len, q0=seq // ctx_len, nq=1,
                              k0=seq // ctx_len, nk=1, row_split=MLA_HEADS * ctx_len // ATT_GROUP_ROWS,
                              name="attention_ctx")
    return jnp.concatenate([att, att_ctx], axis=1)


def _ssd_kernel(xbc_ref, misc_ref, bias_ref, alog_ref, alogt_ref, dskip_ref, y_ref, h_sc, *, direction):
    i = pl.program_id(1)
    q = CHUNK
    n = SSD_STATE
    pdim = SSD_HEAD_DIM

    @pl.when(i == 0)
    def _():
        h_sc[...] = jnp.zeros(h_sc.shape, F32)

    softplus = lambda v: jnp.maximum(v, 0.0) + jnp.log(1.0 + jnp.exp(-jnp.abs(v)))
    nd = 2 * SSD_HEADS
    lane = lax.broadcasted_iota(jnp.int32, (1, LANES), 1)
    r_i = lax.broadcasted_iota(jnp.int32, (q, q), 0)
    c_i = lax.broadcasted_iota(jnp.int32, (q, q), 1)
    valid = (c_i <= r_i) if direction == 0 else (c_i >= r_i)
    tri = jnp.where(valid, 1.0, 0.0).astype(F32)
    tri_t = jnp.where(valid, 0.0, 1.0).astype(F32) + jnp.where(r_i == c_i, 1.0, 0.0).astype(F32)
    last = q - 1 if direction == 0 else 0

    def one_chunk(r0):
        xbc = xbc_ref[0, r0:r0 + q, :]
        xs = xbc[:, :SSD_INNER]
        bmat = xbc[:, SSD_INNER:SSD_INNER + SSD_GROUPS * n]
        cmat = xbc[:, SSD_INNER + SSD_GROUPS * n:]
        raw = misc_ref[0, r0:r0 + q, :] + bias_ref[...]
        a = softplus(raw) * jnp.where(lane < nd, -jnp.exp(alog_ref[...]), 0.0)
        dt_t = softplus(raw.T[0:nd, :])
        a_t = dt_t * (-jnp.exp(alogt_ref[...]))
        acs = jnp.dot(tri, a, precision=HIGHEST, preferred_element_type=F32)
        acs_t = jnp.dot(a_t, tri_t, precision=HIGHEST, preferred_element_type=F32)
        bt = bmat.T
        xb = xs.astype(BF16)

        cb = [jnp.dot(cmat[:, g * n:(g + 1) * n].astype(BF16), bt[g * n:(g + 1) * n, :].astype(BF16),
                      preferred_element_type=F32) for g in range(SSD_GROUPS)]
        ys = []
        for hd in range(SSD_HEADS):
            j = direction * SSD_HEADS + hd
            g = hd // (SSD_HEADS // SSD_GROUPS)
            col = jnp.broadcast_to(acs[:, j:j + 1], (q, q))
            row = acs_t[j:j + 1, :]
            dt_row = dt_t[j:j + 1, :]
            atot = acs_t[j:j + 1, last:last + 1]
            decay = jnp.exp(jnp.where(valid, col - row, NEG_BIG))
            xh = xb[:, hd * pdim:(hd + 1) * pdim]
            h_prev = h_sc[hd]
            c_g = cmat[:, g * n:(g + 1) * n]
            y = (jnp.dot((cb[g] * decay * dt_row).astype(BF16), xh, preferred_element_type=F32)
                 + jnp.dot((c_g * jnp.exp(col[:, :n])).astype(BF16), h_prev.astype(BF16),
                           preferred_element_type=F32))
            if direction == 0:
                y = y + dskip_ref[:, hd * pdim:(hd + 1) * pdim] * xs[:, hd * pdim:(hd + 1) * pdim]
            ys.append(y)
            bw = (bt[g * n:(g + 1) * n, :] * (jnp.exp(atot - row) * dt_row)).astype(BF16)
            h_sc[hd] = jnp.exp(atot) * h_prev + jnp.dot(bw, xh, preferred_element_type=F32)
        y_ref[0, r0:r0 + q, :] = jnp.concatenate(ys, axis=-1)

    subs = range(xbc_ref.shape[1] // q)
    for sub in (subs if direction == 0 else reversed(subs)):
        one_chunk(sub * q)


def _ssd(xbc_act, misc, bias_pad, alog_pad, dskip_wide, direction, n_lat_chunks):
    b, lc, c = xbc_act.shape
    alog_t = jnp.broadcast_to(alog_pad[0, :2 * SSD_HEADS, None], (2 * SSD_HEADS, LANES))
    blk = SSD_STEP_CHUNKS * CHUNK
    assert n_lat_chunks % SSD_STEP_CHUNKS == 0 and lc % blk == 0
    nc = lc // blk
    n_lat_blocks = n_lat_chunks // SSD_STEP_CHUNKS
    if direction == 0:
        cmap = lambda bi, i: (bi, (i + n_lat_blocks) % nc, 0)
    else:
        cmap = lambda bi, i: (bi, nc - 1 - i, 0)
    const2 = lambda bi, i: (0, 0)
    kern = functools.partial(_ssd_kernel, direction=direction)
    return pl.pallas_call(
        kern,
        grid=(b, nc),
        in_specs=[pl.BlockSpec((1, blk, c), cmap),
                  pl.BlockSpec((1, blk, LANES), cmap),
                  pl.BlockSpec((1, LANES), const2),
                  pl.BlockSpec((1, LANES), const2),
                  pl.BlockSpec((2 * SSD_HEADS, LANES), const2),
                  pl.BlockSpec((1, SSD_INNER), const2)],
        out_specs=pl.BlockSpec((1, blk, SSD_INNER), cmap),
        out_shape=jax.ShapeDtypeStruct((b, lc, SSD_INNER), F32),
        scratch_shapes=[pltpu.VMEM((SSD_HEADS, SSD_STATE, SSD_HEAD_DIM), F32)],
        compiler_params=_params(("parallel", "arbitrary")),
        name=f"ssd_dir{direction}",
    )(xbc_act, misc, bias_pad, alog_pad, alog_t, dskip_wide)


def _mixout_kernel(att_ref, y0_ref, y1_ref, z_ref, x_ref, c_ref, mod_ref, sg_ref, wout_ref, g2_ref,
                   rwh_ref, rwl_ref, rb_ref, xn_ref, h2_ref, gate_ref, cnt_ref, *, n_lat_tiles):
    half = MLA_HEADS * V_DIM
    nr = x_ref.shape[1] // MIX_SLABS
    is_lat = pl.program_id(1) < n_lat_tiles

    def slab(rs):
        y = (y0_ref[0, rs, :] + y1_ref[0, rs, :]) * _silu(z_ref[0, rs, :])
        ssd = _rms(y, sg_ref[...])
        o = (jnp.dot(att_ref[0, rs, :], wout_ref[0:half, :], preferred_element_type=F32)
             + jnp.dot(ssd.astype(BF16), wout_ref[half:, :], preferred_element_type=F32))
        xn = jnp.where(is_lat, x_ref[0, rs, :], c_ref[0, rs, :]) + mod_ref[0, 0, 2:3, :] * o
        xn_ref[0, rs, :] = xn
        h2 = _rms(xn, g2_ref[...]) * (1.0 + mod_ref[0, 0, 4:5, :]) + mod_ref[0, 0, 3:4, :]
        h2_hi = h2.astype(BF16)
        h2_ref[0, rs, :] = h2_hi

        h2_lo = (h2 - h2_hi.astype(F32)).astype(BF16)
        logits = (jnp.dot(h2_hi, rwh_ref[...], preferred_element_type=F32)
                  + jnp.dot(h2_lo, rwh_ref[...], preferred_element_type=F32)
                  + jnp.dot(h2_hi, rwl_ref[...], preferred_element_type=F32)) + rb_ref[...]
        lane = lax.broadcasted_iota(jnp.int32, logits.shape, 1)
        is_grp = jnp.logical_and(lane >= N_EXPERTS, lane < N_EXPERTS + N_GROUPS)
        l1 = jnp.where(is_grp, logits, NEG_BIG)
        m1 = jnp.max(l1, axis=-1, keepdims=True)
        p_top = 1.0 / jnp.sum(jnp.exp(l1 - m1), axis=-1, keepdims=True)
        g_idx = jnp.min(jnp.where(jnp.logical_and(is_grp, l1 == m1), lane - N_EXPERTS, N_GROUPS),
                        axis=-1, keepdims=True)
        sel = jnp.logical_and(lane < N_EXPERTS, (lane // EXPERTS_PER_GROUP) == g_idx)
        l2 = jnp.where(sel, logits, NEG_BIG)
        m2 = jnp.max(l2, axis=-1, keepdims=True)
        i_first = jnp.min(jnp.where(jnp.logical_and(sel, l2 == m2), lane, LANES), axis=-1, keepdims=True)
        l3 = jnp.where(lane == i_first, NEG_BIG, l2)
        m3 = jnp.max(l3, axis=-1, keepdims=True)
        i_second = jnp.min(jnp.where(jnp.logical_and(sel, l3 == m3), lane, LANES), axis=-1, keepdims=True)
        e2 = jnp.exp(m3 - m2)
        inv = p_top / (1.0 + e2)
        gate = jnp.where(lane == i_first, inv, jnp.where(lane == i_second, e2 * inv, 0.0))
        grp_hot = jnp.where(jnp.logical_and(is_grp, lane - N_EXPERTS == g_idx), 1.0, 0.0)
        gate_ref[0, rs, :] = gate + grp_hot
        return jnp.sum(grp_hot, axis=0, keepdims=True)

    counts = [slab(slice(k * nr, (k + 1) * nr)) for k in range(MIX_SLABS)]
    cnt_ref[0, 0] = jnp.broadcast_to(sum(counts), (SUBLANES, LANES))


def _mixout(att, y0, y1, z, lat, ctxa, ctx_blk, modsel, sg, wout, g2, rw, rb, n_lat_tiles):
    rw_hi = rw.astype(BF16)
    rw_lo = (rw - rw_hi.astype(F32)).astype(BF16)
    b, _, d = lat.shape
    tm = TOKEN_TILE
    nt = n_lat_tiles + 1
    lc = nt * tm
    tok = lambda w: pl.BlockSpec((1, tm, w), lambda bi, i: (bi, i, 0))
    const2 = lambda bi, i: (0, 0)
    return pl.pallas_call(
        functools.partial(_mixout_kernel, n_lat_tiles=n_lat_tiles),
        grid=(b, nt),
        in_specs=[pl.BlockSpec((1, tm, SSD_INNER), lambda bi, i: (bi, jnp.minimum(i, att.shape[1] // tm - 1), 0)),
                  tok(SSD_INNER), tok(SSD_INNER), tok(SSD_INNER),
                  pl.BlockSpec((1, tm, d), lambda bi, i: (bi, jnp.minimum(i, n_lat_tiles - 1), 0)),
                  pl.BlockSpec((1, tm, d), lambda bi, i: (bi, ctx_blk, 0)),
                  pl.BlockSpec((1, 1, SUBLANES, d), lambda bi, i: (bi, i // n_lat_tiles, 0, 0)),
                  pl.BlockSpec((1, SSD_INNER), const2),
                  pl.BlockSpec(wout.shape, const2),
                  pl.BlockSpec((1, d), const2),
                  pl.BlockSpec((d, LANES), const2),
                  pl.BlockSpec((d, LANES), const2),
                  pl.BlockSpec((1, LANES), const2)],
        out_specs=[tok(d), tok(d), tok(LANES),
                   pl.BlockSpec((1, 1, SUBLANES, LANES), lambda bi, i: (bi, i, 0, 0))],
        out_shape=[jax.ShapeDtypeStruct((b, lc, d), F32),
                   jax.ShapeDtypeStruct((b, lc, d), BF16),
                   jax.ShapeDtypeStruct((b, lc, LANES), F32),
                   jax.ShapeDtypeStruct((b, nt, SUBLANES, LANES), F32)],
        compiler_params=_params(("parallel", "parallel")),
        name="mixout",
    )(att, y0, y1, z, lat, ctxa, modsel, sg, wout, g2, rw_hi, rw_lo, rb)


def _moe_kernel(off_ref, h2_ref, gate_ref, wg_ref, wu_ref, wd_ref, y_ref, hs_sc, gs_sc, ys_sc, pt_sc,
                *, tmm, nt, win):
    bi = pl.program_id(0)
    t = pl.program_id(1)
    s = pl.program_id(2)
    base = (bi * nt + t) * SUBLANES
    nblk = tmm // MOE_BLOCK

    @pl.when(s == 0)
    def _():
        gate = gate_ref[0]
        lane = lax.broadcasted_iota(jnp.int32, (tmm, LANES), 1)
        hot = jnp.where(jnp.logical_and(lane >= GRP_LANE, lane < GRP_LANE + N_GROUPS), gate, 0.0)
        r_i = lax.broadcasted_iota(jnp.int32, (MOE_BLOCK, MOE_BLOCK), 0)
        c_i = lax.broadcasted_iota(jnp.int32, (MOE_BLOCK, MOE_BLOCK), 1)
        tri = jnp.where(c_i < r_i, 1.0, 0.0).astype(BF16)
        run = jnp.zeros((1, LANES), F32)
        cums = []
        for blk in range(nblk):
            hb = hot[blk * MOE_BLOCK:(blk + 1) * MOE_BLOCK]
            cums.append(jnp.dot(tri, hb.astype(BF16), preferred_element_type=F32) + run)
            run = run + jnp.sum(hb, axis=0, keepdims=True)
        cum = jnp.concatenate(cums, axis=0)
        lane1 = lax.broadcasted_iota(jnp.int32, (1, LANES), 1)
        offv = jnp.zeros((1, LANES), F32)
        for g in range(N_GROUPS):
            offv = jnp.where(lane1 == GRP_LANE + g, off_ref[base + g].astype(F32), offv)
        pos = jnp.sum(hot * (cum + offv), axis=-1, keepdims=True)
        dst = lax.broadcasted_iota(jnp.int32, (tmm, tmm), 1).astype(F32)
        pt_sc[...] = jnp.where(pos == dst, 1.0, 0.0).astype(BF16)
        pos_row = jnp.broadcast_to(pos, (tmm, LANES)).T[0:1, :]
        src = lax.broadcasted_iota(jnp.int32, (tmm, tmm), 0).astype(F32)
        perm = jnp.where(pos_row == src, 1.0, 0.0).astype(BF16)
        hs_sc[...] = jnp.dot(perm, h2_ref[0], preferred_element_type=F32).astype(BF16)
        g_hi = gate.astype(BF16)
        g_lo = (gate - g_hi.astype(F32)).astype(BF16)
        g2 = jnp.dot(perm, jnp.concatenate([g_hi, g_lo], axis=-1), preferred_element_type=F32)
        gs_sc[...] = g2[:, :LANES] + g2[:, LANES:]
        ys_sc[...] = jnp.zeros(ys_sc.shape, F32)

    grp = s // (EXPERTS_PER_GROUP // MOE_STEP_EXPERTS)
    lo = off_ref[base + grp]
    hi = off_ref[base + grp + 1]
    start = (lo // BF16_ROWS) * BF16_ROWS
    lane_b = lax.broadcasted_iota(jnp.int32, (win, LANES), 1)
    row_b = lax.broadcasted_iota(jnp.int32, (win, 1), 0)

    def window(k, carry):
        first = start + k * win
        r0 = pl.multiple_of(jnp.minimum(first, tmm - win), BF16_ROWS)
        hb = hs_sc[pl.ds(r0, win), :]
        gb = gs_sc[pl.ds(r0, win), :]
        live = r0 + row_b >= first
        acc = jnp.zeros((win, ys_sc.shape[1]), F32)
        for e in range(MOE_STEP_EXPERTS):
            gcol = jnp.sum(jnp.where(lane_b == s * MOE_STEP_EXPERTS + e, gb, 0.0), axis=-1, keepdims=True)
            gcol = jnp.where(live, gcol, 0.0)
            hg = jnp.dot(hb, wg_ref[e], preferred_element_type=F32)
            hu = jnp.dot(hb, wu_ref[e], preferred_element_type=F32)
            act = (_silu(hg) * hu * gcol).astype(BF16)
            acc = acc + jnp.dot(act, wd_ref[e], preferred_element_type=F32)
        ys_sc[pl.ds(r0, win), :] += acc
        return carry

    lax.fori_loop(0, (hi - start + win - 1) // win, window, 0)

    @pl.when(s == N_EXPERTS // MOE_STEP_EXPERTS - 1)
    def _():
        y_ref[0] = jnp.dot(pt_sc[...], ys_sc[...].astype(BF16), preferred_element_type=F32).astype(BF16)


def _moe(h2, gate, cnt, wg, wu, wd, rows, tmm):
    b, _, d = h2.shape
    nt = rows // tmm
    per = tmm // TOKEN_TILE
    n = cnt[:, :nt * per, 0, GRP_LANE:GRP_LANE + N_GROUPS].reshape(b, nt, per, N_GROUPS).sum(axis=2)
    ends = jnp.cumsum(n.astype(jnp.int32), axis=-1)
    zeros = jnp.zeros((b, nt, 1), jnp.int32)
    off = jnp.concatenate([zeros, ends] + [zeros] * (SUBLANES - 1 - N_GROUPS), axis=-1).reshape(-1)
    win = min(tmm, -(-(tmm * 3 // 10) // 64) * 64)
    kern = functools.partial(_moe_kernel, tmm=tmm, nt=nt, win=win)
    tok = lambda w: pl.BlockSpec((1, tmm, w), lambda bi, t, s, o: (bi, t, 0))
    wspec = lambda r, c: pl.BlockSpec((MOE_STEP_EXPERTS, r, c), lambda bi, t, s, o: (s, 0, 0))
    return pl.pallas_call(
        kern,
        grid_spec=pltpu.PrefetchScalarGridSpec(
            num_scalar_prefetch=1,
            grid=(b, nt, N_EXPERTS // MOE_STEP_EXPERTS),
            in_specs=[tok(d), tok(LANES), wspec(d, EXPERT_FF), wspec(d, EXPERT_FF), wspec(EXPERT_FF, d)],
            out_specs=tok(d),
            scratch_shapes=[pltpu.VMEM((tmm, d), BF16), pltpu.VMEM((tmm, LANES), F32),
                            pltpu.VMEM((tmm, d), F32), pltpu.VMEM((tmm, tmm), BF16)]),
        out_shape=jax.ShapeDtypeStruct((b, rows, d), BF16),
        compiler_params=_params(("parallel", "parallel", "arbitrary"), vmem_mb=56),
        name=f"moe_t{tmm}",
    )(off, h2, gate, wg, wu, wd)


def _resid_kernel(xn_ref, y_ref, mod_ref, fg_ref, o_ref, *, tm, seq, final):
    row = pl.program_id(1) * tm + lax.broadcasted_iota(jnp.int32, (tm, 1), 0)
    g2 = jnp.where(row >= seq, mod_ref[0, 1, 5:6, :], mod_ref[0, 0, 5:6, :])
    out = xn_ref[0] + g2 * y_ref[0].astype(F32)
    if final:
        out = _rms(out, fg_ref[...])
    o_ref[0] = out


def _resid(xn, y, modsel, fg, rows, seq, final):
    b, _, d = xn.shape
    tm = _pick(rows, (1280, 1024, 768, 512, 256))
    tok = pl.BlockSpec((1, tm, d), lambda bi, i: (bi, i, 0))
    return pl.pallas_call(
        functools.partial(_resid_kernel, tm=tm, seq=seq, final=final),
        grid=(b, rows // tm),
        in_specs=[tok, tok,
                  pl.BlockSpec((1, 2, SUBLANES, d), lambda bi, i: (bi, 0, 0, 0)),
                  pl.BlockSpec((1, d), lambda bi, i: (0, 0))],
        out_specs=tok,
        out_shape=jax.ShapeDtypeStruct((b, rows, d), F32),
        compiler_params=_params(("parallel", "parallel")),
        name="resid_final" if final else "resid",
    )(xn, y, modsel, fg)


def _rope_tables(seq, ctx_len):
    rows = seq // GRID_W
    row = jnp.repeat(jnp.arange(rows), GRID_W)
    col = jnp.tile(jnp.arange(GRID_W), rows)
    inv_freq = ROPE_THETA ** (-jnp.arange(ROPE_DIM // 4, dtype=F32) / (ROPE_DIM // 4))
    ang = jnp.concatenate([row[:, None] * inv_freq, col[:, None] * inv_freq], axis=-1)
    cos, sin = jnp.cos(ang), jnp.sin(ang)
    c32 = jnp.concatenate([cos, cos], axis=-1)
    s32 = jnp.concatenate([-sin, sin], axis=-1)
    c32 = jnp.concatenate([c32, jnp.ones((ctx_len, ROPE_DIM), F32)], axis=0)
    s32 = jnp.concatenate([s32, jnp.zeros((ctx_len, ROPE_DIM), F32)], axis=0)
    pad = ((0, 0), (0, LANES - ROPE_DIM))
    return jnp.concatenate([jnp.pad(c32, pad), jnp.pad(s32, pad)], axis=-1)


def _swap_halves(w):
    lead = w.shape[:-1]
    w4 = w.reshape(lead + (-1, 2, ROPE_DIM // 2))
    return w4[..., ::-1, :].reshape(w.shape)


def kernel(x, c, ctx, c_ctx, w_mod, b_mod, norm1_g, norm2_g, w_in, q_norm_g, w_qb, kv_norm_g, w_kvb,
           conv_w, conv_b, a_log, dt_bias, d_skip, ssd_norm_g, w_out, router_w1, router_b1,
           router_w2, router_b2, w_gate, w_up, w_down, final_g):
    b, seq, d = x.shape
    ctx_len = ctx.shape[1]
    depth = w_mod.shape[0]
    assert ctx_len == TOKEN_TILE and seq % TOKEN_TILE == 0 and seq % GRID_W == 0
    n_lat_tiles = seq // TOKEN_TILE
    n_lat_chunks = seq // CHUNK

    lat, ctxa, ctx_blk = x, ctx, 0
    tab = _rope_tables(seq, ctx_len)

    cvec = jnp.zeros((SUBLANES, d), F32).at[:b].set(c).at[b].set(c_ctx)
    mod = _modulation(cvec, w_mod, b_mod).reshape(depth, SUBLANES, 6, d)

    for l in range(depth):
        last_layer = l == depth - 1
        ml = mod[l]
        modsel = jnp.stack([ml[:b], jnp.broadcast_to(ml[b], (b, 6, d))], axis=1)
        modsel = jnp.pad(modsel, ((0, 0), (0, 0), (0, SUBLANES - 6), (0, 0)))

        w = w_in[l]
        o_kva, o_kr = Q_LORA, Q_LORA + KV_LORA
        o_z = o_kr + ROPE_DIM
        o_xbc, o_dt = o_z + SSD_INNER, o_z + SSD_INNER + XBC_DIM
        w_kr = w[:, o_kr:o_z]
        lane_pad = lambda t: jnp.pad(t, [(0, 0)] * (t.ndim - 1) + [(0, LANES - t.shape[-1])])
        win = jnp.concatenate([w[:, :o_kr], w[:, o_z:o_dt], lane_pad(w[:, o_dt:]), lane_pad(w_kr),
                               lane_pad(_swap_halves(w_kr))], axis=-1).astype(BF16)
        wq3 = w_qb[l].reshape(Q_LORA, MLA_HEADS, NOPE_DIM + ROPE_DIM)
        wq_rope = wq3[:, :, NOPE_DIM:]
        wq = jnp.concatenate([wq3[:, :, :NOPE_DIM].reshape(Q_LORA, MLA_HEADS * NOPE_DIM),
                              lane_pad(wq_rope).reshape(Q_LORA, MLA_HEADS * LANES),
                              lane_pad(_swap_halves(wq_rope)).reshape(Q_LORA, MLA_HEADS * LANES)],
                             axis=-1).astype(BF16)
        wkv3 = w_kvb[l].reshape(KV_LORA, MLA_HEADS, NOPE_DIM + V_DIM)
        wkbt = jnp.transpose(wkv3[:, :, :NOPE_DIM], (1, 2, 0)).astype(BF16)
        wvb = jnp.transpose(wkv3[:, :, NOPE_DIM:], (1, 0, 2)).astype(BF16)

        cw = jnp.pad(conv_w[l], ((0, SUBLANES - CONV_W), (0, 0)))
        q, kv, kvt, z, xbc_act, misc = _inproj(lat, ctxa, ctx_blk, modsel, norm1_g[l][None], win, q_norm_g[l][None],
                                               kv_norm_g[l][None], wq, wkbt, tab, cw, conv_b[l][None],
                                               n_lat_tiles)
        att = _attention(q, kv, kvt, wvb, seq, ctx_len, not last_layer)

        bias_pad = jnp.pad(dt_bias[l].reshape(1, -1), ((0, 0), (0, LANES - 2 * SSD_HEADS)))
        alog_pad = jnp.pad(a_log[l].reshape(1, -1), ((0, 0), (0, LANES - 2 * SSD_HEADS)))
        dskip_wide = jnp.repeat(d_skip[l], SSD_HEAD_DIM)[None]
        y0 = _ssd(xbc_act, misc, bias_pad, alog_pad, dskip_wide, 0, n_lat_chunks)
        y1 = _ssd(xbc_act, misc, bias_pad, alog_pad, dskip_wide, 1, n_lat_chunks)

        rw = jnp.pad(jnp.concatenate([router_w2[l], router_w1[l]], axis=-1),
                     ((0, 0), (0, LANES - N_EXPERTS - N_GROUPS)))
        rb = jnp.pad(jnp.concatenate([router_b2[l], router_b1[l]])[None],
                     ((0, 0), (0, LANES - N_EXPERTS - N_GROUPS)))
        xn, h2, gate, cnt = _mixout(att, y0, y1, z, lat, ctxa, ctx_blk, modsel, ssd_norm_g[l][None], w_out[l].astype(BF16),
                                    norm2_g[l][None], rw, rb, n_lat_tiles)
        rows = seq if last_layer else seq + ctx_len
        tmm = _pick(rows, (1280, 1024, 768, 512, 256))
        y = _moe(h2, gate, cnt, w_gate[l].astype(BF16), w_up[l].astype(BF16), w_down[l].astype(BF16),
                 rows, tmm)
        lat = _resid(xn, y, modsel, final_g[None], rows, seq, last_layer)
        ctxa, ctx_blk = lat, n_lat_tiles
    return lat
```
